```python
import jax
import jax.numpy as jnp
from jax import lax
import numpy as np

D_MODEL = 4096
BATCH = 2
SEQ = 4096
DEPTH = 4

N_MIXERS = 2
N_MEM = 256
MIX_WIDTH = D_MODEL
MEM_HEADS = 4
MEM_WIDTH = MIX_WIDTH // 4
MEM_HEAD_DIM = MEM_WIDTH // MEM_HEADS
MAIN_WIDTH = MIX_WIDTH - MEM_WIDTH
MLA_NOPE_DIM = 128
MLA_ROPE_DIM = 64
MLA_V_DIM = 128
MLA_HEADS = MAIN_WIDTH // MLA_V_DIM
MLA_Q_RANK = max(256, -(-(1536 * D_MODEL // 7168) // 256) * 256)
MLA_KV_RANK = 512
ROPE_THETA = 10000.0
ATTN_BLOCK = 128
MLSTM_V_DIM = 512
MLSTM_QK_DIM = MLSTM_V_DIM // 2
MLSTM_HEADS = MAIN_WIDTH // MLSTM_V_DIM
MLSTM_QK_WIDTH = MLSTM_HEADS * MLSTM_QK_DIM
MLSTM_V_WIDTH = MLSTM_HEADS * MLSTM_V_DIM
MLSTM_CHUNK = 128
CONV_WIDTH = 3
IGATE_CAP = 15.0
FGATE_BIAS = 3.0
STAB_INIT = -1e30
D_FF = -(-(8 * D_MODEL) // (3 * 256)) * 256
EPS = 1e-6
N_MLA_LAYERS = (DEPTH + N_MIXERS - 1) // N_MIXERS
N_MLSTM_LAYERS = DEPTH // N_MIXERS
MLA_IN_COLS = MLA_Q_RANK + MLA_KV_RANK + MLA_ROPE_DIM + MEM_WIDTH
MLSTM_IN_COLS = 2 * MLSTM_QK_WIDTH + 2 * MLSTM_V_WIDTH + 4 * MLSTM_HEADS + MEM_WIDTH

kernel_name = 'hybrid_mla_mlstm_memory_encoder'


def rms_norm(x, g):
    xf = x.astype(jnp.float32)
    y = xf * lax.rsqrt(jnp.mean(xf * xf, axis=-1, keepdims=True) + EPS)
    return (y * g.astype(jnp.float32)).astype(x.dtype)


def rope(x, cos, sin):
    half = x.shape[-1] // 2
    x1, x2 = x[..., :half], x[..., half:]
    return jnp.concatenate([x1 * cos - x2 * sin, x2 * cos + x1 * sin], axis=-1)


def mla_attention(c_q, c_kv, k_rope, q_norm, kv_norm, w_uq, w_ukv, cos, sin):
    B, S, _ = c_q.shape
    q = (rms_norm(c_q, q_norm) @ w_uq).reshape(B, S, MLA_HEADS, MLA_NOPE_DIM + MLA_ROPE_DIM)
    q_nope = q[..., :MLA_NOPE_DIM]
    q_rope = rope(q[..., MLA_NOPE_DIM:], cos[:, :, None, :], sin[:, :, None, :])
    kv = (rms_norm(c_kv, kv_norm) @ w_ukv).reshape(B, S, MLA_HEADS, MLA_NOPE_DIM + MLA_V_DIM)
    k_nope, v = kv[..., :MLA_NOPE_DIM], kv[..., MLA_NOPE_DIM:]
    k_rope = rope(k_rope, cos, sin)
    nb = S // ATTN_BLOCK
    qn_blocks = q_nope.reshape(B, nb, ATTN_BLOCK, MLA_HEADS, MLA_NOPE_DIM).transpose(1, 0, 2, 3, 4)
    qr_blocks = q_rope.reshape(B, nb, ATTN_BLOCK, MLA_HEADS, MLA_ROPE_DIM).transpose(1, 0, 2, 3, 4)
    scale = (MLA_NOPE_DIM + MLA_ROPE_DIM) ** -0.5

    def attend(blk):
        qn, qr = blk
        s = jnp.einsum('bqhd,bkhd->bhqk', qn, k_nope) + jnp.einsum('bqhr,bkr->bhqk', qr, k_rope)
        p = jax.nn.softmax(s.astype(jnp.float32) * scale, axis=-1).astype(v.dtype)
        return jnp.einsum('bhqk,bkhd->bqhd', p, v)

    o = lax.map(attend, (qn_blocks, qr_blocks))
    return o.transpose(1, 0, 2, 3, 4).reshape(B, S, MLA_HEADS * MLA_V_DIM)


def mlstm_direction(q, k, v, log_i, log_f):
    B, H, S, dk = q.shape
    dv = v.shape[-1]
    nc = S // MLSTM_CHUNK

    def chunks(t):
        t = t.reshape((B, H, nc, MLSTM_CHUNK) + t.shape[3:])
        return jnp.moveaxis(t, 2, 0)

    tril = jnp.tril(jnp.ones((MLSTM_CHUNK, MLSTM_CHUNK), dtype=bool))

    def step(carry, inp):
        C, n, m = carry
        qc, kc, vc, ic, fc = inp
        b = jnp.cumsum(fc, axis=-1)
        D = jnp.where(tril, b[..., :, None] - b[..., None, :] + ic[..., None, :], -jnp.inf)
        inter = b + m[..., None]
        m_row = jnp.maximum(jnp.max(D, axis=-1), inter)
        s = jnp.einsum('bhtd,bhjd->bhtj', qc, kc) * jnp.exp(D - m_row[..., None])
        w_inter = jnp.exp(inter - m_row)
        num = jnp.einsum('bhtj,bhjv->bhtv', s, vc) + w_inter[..., None] * jnp.einsum('bhtd,bhdv->bhtv', qc, C)
        den = jnp.sum(s, axis=-1) + w_inter * jnp.einsum('bhtd,bhd->bht', qc, n)
        h = num / jnp.maximum(jnp.abs(den), jnp.exp(-m_row))[..., None]
        b_end = b[..., -1]
        w_tok = b_end[..., None] - b + ic
        m_new = jnp.maximum(b_end + m, jnp.max(w_tok, axis=-1))
        decay = jnp.exp(b_end + m - m_new)
        wk = kc * jnp.exp(w_tok - m_new[..., None])[..., None]
        C = decay[..., None, None] * C + jnp.einsum('bhjd,bhjv->bhdv', wk, vc)
        n = decay[..., None] * n + jnp.sum(wk, axis=2)
        return (C, n, m_new), h

    init = (jnp.zeros((B, H, dk, dv), jnp.float32), jnp.zeros((B, H, dk), jnp.float32),
            jnp.full((B, H), STAB_INIT, jnp.float32))
    _, h = lax.scan(step, init, (chunks(q), chunks(k), chunks(v), chunks(log_i), chunks(log_f)))
    return jnp.moveaxis(h, 0, 2).reshape(B, H, S, dv)


def mlstm_mixer(qk_pre, v, o_pre, g_pre, conv_w, gate_b, head_norm):
    B, S, _ = v.shape
    qk = lax.conv_general_dilated(qk_pre, conv_w[:, None, :].astype(qk_pre.dtype), window_strides=(1,),
                                  padding='SAME', dimension_numbers=('NWC', 'WIO', 'NWC'),
                                  feature_group_count=2 * MLSTM_QK_WIDTH)
    qk = jax.nn.silu(qk).astype(jnp.float32)

    def heads(t, d):
        return t.reshape(B, S, MLSTM_HEADS, d).transpose(0, 2, 1, 3)

    q = heads(qk[..., :MLSTM_QK_WIDTH], MLSTM_QK_DIM) * (MLSTM_QK_DIM ** -0.5)
    k = heads(qk[..., MLSTM_QK_WIDTH:], MLSTM_QK_DIM)
    vh = heads(v.astype(jnp.float32), MLSTM_V_DIM)
    g = g_pre.astype(jnp.float32).reshape(B, S, 4, MLSTM_HEADS) + gate_b.astype(jnp.float32)
    g = g.transpose(2, 0, 3, 1)
    i_fw = IGATE_CAP * jnp.tanh(g[0] / IGATE_CAP)
    lf_fw = jax.nn.log_sigmoid(g[1])
    i_bw = IGATE_CAP * jnp.tanh(g[2] / IGATE_CAP)
    lf_bw = jax.nn.log_sigmoid(g[3])

    def flip(t):
        return jnp.flip(t, axis=2)

    h = mlstm_direction(q, k, vh, i_fw, lf_fw) + flip(
        mlstm_direction(flip(q), flip(k), flip(vh), flip(i_bw), flip(lf_bw)))
    h = rms_norm(h.transpose(0, 2, 1, 3), head_norm.reshape(MLSTM_HEADS, MLSTM_V_DIM))
    o = jax.nn.sigmoid(o_pre.astype(jnp.float32)).reshape(B, S, MLSTM_HEADS, MLSTM_V_DIM)
    return (h * o).reshape(B, S, MLSTM_V_WIDTH).astype(v.dtype)


def memory_attention(q_mem, mem_k, mem_v):
    B, S, _ = q_mem.shape
    q = q_mem.reshape(B, S, MEM_HEADS, MEM_HEAD_DIM)
    s = jnp.einsum('bshd,bmhd->bhsm', q, mem_k).astype(jnp.float32) * (MEM_HEAD_DIM ** -0.5)
    p = jax.nn.softmax(s, axis=-1).astype(mem_v.dtype)
    return jnp.einsum('bhsm,bmhd->bshd', p, mem_v).reshape(B, S, MEM_WIDTH)


def setup_inputs(seed: int = 0) -> dict:
    key = jax.random.key(seed)
    ks = jax.random.split(key, 24)
    f32 = jnp.float32

    def nrm(k, shape, scale):
        return jax.random.normal(k, shape, f32) * scale

    def gain(k, shape):
        return 1.0 + 0.02 * jax.random.normal(k, shape, f32)

    res_scale = (2 * DEPTH) ** -0.5
    offsets = jax.random.randint(ks[1], (BATCH, 1), 0, 1024, dtype=jnp.int32)
    positions = offsets + jnp.arange(SEQ, dtype=jnp.int32)[None, :]
    gate_offsets = jnp.array([0.0, FGATE_BIAS, 0.0, FGATE_BIAS], f32)[None, :, None]
    return {
        'x': nrm(ks[0], (BATCH, SEQ, D_MODEL), 1.0),
        'positions': positions,
        'mem': nrm(ks[2], (BATCH, N_MEM, D_MODEL), 1.0),
        'attn_norm': gain(ks[3], (DEPTH, D_MODEL)),
        'ffn_norm': gain(ks[4], (DEPTH, D_MODEL)),
        'mem_norm': gain(ks[5], (D_MODEL,)),
        'final_norm': gain(ks[6], (D_MODEL,)),
        'mla_w_in': nrm(ks[7], (N_MLA_LAYERS, D_MODEL, MLA_IN_COLS), D_MODEL ** -0.5),
        'mla_q_norm': gain(ks[8], (N_MLA_LAYERS, MLA_Q_RANK)),
        'mla_kv_norm': gain(ks[9], (N_MLA_LAYERS, MLA_KV_RANK)),
        'mla_w_uq': nrm(ks[10], (N_MLA_LAYERS, MLA_Q_RANK, MLA_HEADS * (MLA_NOPE_DIM + MLA_ROPE_DIM)), MLA_Q_RANK ** -0.5),
        'mla_w_ukv': nrm(ks[11], (N_MLA_LAYERS, MLA_KV_RANK, MLA_HEADS * (MLA_NOPE_DIM + MLA_V_DIM)), MLA_KV_RANK ** -0.5),
        'mlstm_w_in': nrm(ks[12], (N_MLSTM_LAYERS, D_MODEL, MLSTM_IN_COLS), D_MODEL ** -0.5),
        'mlstm_conv_w': nrm(ks[13], (N_MLSTM_LAYERS, CONV_WIDTH, 2 * MLSTM_QK_WIDTH), CONV_WIDTH ** -0.5),
        'mlstm_gate_b': gate_offsets + nrm(ks[14], (N_MLSTM_LAYERS, 4, MLSTM_HEADS), 0.1),
        'mlstm_head_norm': gain(ks[15], (N_MLSTM_LAYERS, MLSTM_V_WIDTH)),
        'w_mem_kv': nrm(ks[16], (DEPTH, D_MODEL, 2 * MEM_WIDTH), D_MODEL ** -0.5),
        'w_out': nrm(ks[17], (DEPTH, MIX_WIDTH, D_MODEL), MIX_WIDTH ** -0.5 * res_scale),
        'w_gu': nrm(ks[18], (DEPTH, D_MODEL, 2 * D_FF), D_MODEL ** -0.5),
        'w_down': nrm(ks[19], (DEPTH, D_FF, D_MODEL), D_FF ** -0.5 * res_scale),
    }


def reference(x, positions, mem, attn_norm, ffn_norm, mem_norm, final_norm, mla_w_in, mla_q_norm,
              mla_kv_norm, mla_w_uq, mla_w_ukv, mlstm_w_in, mlstm_conv_w, mlstm_gate_b, mlstm_head_norm,
              w_mem_kv, w_out, w_gu, w_down):
    B, S, _ = x.shape
    n_mem = mem.shape[1]
    inv_freq = 1.0 / (ROPE_THETA ** (jnp.arange(0, MLA_ROPE_DIM, 2, dtype=jnp.float32) / MLA_ROPE_DIM))
    ang = positions.astype(jnp.float32)[..., None] * inv_freq
    cos = jnp.cos(ang).astype(x.dtype)
    sin = jnp.sin(ang).astype(x.dtype)
    mem_n = rms_norm(mem, mem_norm)
    h = x
    for i in range(DEPTH):
        j = i // N_MIXERS
        hn = rms_norm(h, attn_norm[i])
        mkv = (mem_n @ w_mem_kv[i]).reshape(B, n_mem, 2, MEM_HEADS, MEM_HEAD_DIM)
        if i % N_MIXERS == 0:
            p = hn @ mla_w_in[j]
            a0 = MLA_Q_RANK
            a1 = a0 + MLA_KV_RANK
            a2 = a1 + MLA_ROPE_DIM
            main = mla_attention(p[..., :a0], p[..., a0:a1], p[..., a1:a2], mla_q_norm[j], mla_kv_norm[j],
                                 mla_w_uq[j], mla_w_ukv[j], cos, sin)
            q_mem = p[..., a2:]
        else:
            p = hn @ mlstm_w_in[j]
            b0 = 2 * MLSTM_QK_WIDTH
            b1 = b0 + MLSTM_V_WIDTH
            b2 = b1 + MLSTM_V_WIDTH
            b3 = b2 + 4 * MLSTM_HEADS
            main = mlstm_mixer(p[..., :b0], p[..., b0:b1], p[..., b1:b2], p[..., b2:b3], mlstm_conv_w[j],
                               mlstm_gate_b[j], mlstm_head_norm[j])
            q_mem = p[..., b3:]
        mem_out = memory_attention(q_mem, mkv[:, :, 0], mkv[:, :, 1])
        h = h + jnp.concatenate([main, mem_out], axis=-1) @ w_out[i]
        gu = rms_norm(h, ffn_norm[i]) @ w_gu[i]
        h = h + (jax.nn.silu(gu[..., :D_FF]) * gu[..., D_FF:]) @ w_down[i]
    return rms_norm(h, final_norm)
```

```python
import functools

import jax
import jax.numpy as jnp
from jax import lax
from jax.experimental import pallas as pl
from jax.experimental.pallas import tpu as pltpu

F32 = jnp.float32
BF16 = jnp.bfloat16

EPS = 1e-6
ROPE_THETA = 10000.0
MLA_NOPE_DIM = 128
MLA_ROPE_DIM = 64
MLA_V_DIM = 128
MLA_HEAD_COLS = 2 * MLA_NOPE_DIM
MLSTM_V_DIM = 512
MLSTM_QK_DIM = MLSTM_V_DIM // 2
MLSTM_CHUNK = 128
MEM_HEADS = 4
IGATE_CAP = 15.0
STAB_INIT = -1e30
N_MIXERS = 2

V7X_VMEM_BYTES = 64 * 1024 * 1024
VMEM_LIMIT_BYTES = V7X_VMEM_BYTES - 8 * 1024 * 1024


def _params(*semantics):
    return pltpu.CompilerParams(dimension_semantics=semantics, vmem_limit_bytes=VMEM_LIMIT_BYTES)


def _pick(n, candidates):
    for c in candidates:
        if n % c == 0:
            return c
    return n


def _dot(a, b):
    return jnp.dot(a, b, preferred_element_type=F32)


def _dot_nt(a, b):
    return lax.dot_general(a, b, (((1,), (1,)), ((), ())), preferred_element_type=F32)


def _dot_tn(a, b):
    return lax.dot_general(a, b, (((0,), (0,)), ((), ())), preferred_element_type=F32)


def _rmsnorm_body(x_ref, g_ref, o_ref):
    x = x_ref[...].astype(F32)
    ms = jnp.mean(x * x, axis=-1, keepdims=True)
    o_ref[...] = (x * lax.rsqrt(ms + EPS) * g_ref[...]).astype(o_ref.dtype)


def _rmsnorm(x, g, *, width, col_block=0, out_dtype=BF16, name="rmsnorm"):
    t = x.shape[0]
    tm = _pick(t, (256, 128, 64, 8))
    return pl.pallas_call(
        _rmsnorm_body,
        grid=(t // tm,),
        in_specs=[pl.BlockSpec((tm, width), lambda i: (i, col_block)),
                  pl.BlockSpec((1, width), lambda i: (0, 0))],
        out_specs=pl.BlockSpec((tm, width), lambda i: (i, 0)),
        out_shape=jax.ShapeDtypeStruct((t, width), out_dtype),
        compiler_params=_params("parallel"),
        name=name,
    )(x, g.reshape(1, width).astype(F32))


def _mm_plain_body(a_ref, w_ref, o_ref):
    o_ref[...] = _dot(a_ref[...].astype(BF16), w_ref[...]).astype(o_ref.dtype)


def _mm_mul_body(a_ref, w_ref, t_ref, o_ref):
    o_ref[...] = (_dot(a_ref[...].astype(BF16), w_ref[...]) * t_ref[...]).astype(o_ref.dtype)


def _mm_res_body(a_ref, w_ref, r_ref, o_ref):
    o_ref[...] = r_ref[...] + _dot(a_ref[...], w_ref[...])


def _mm_res2_body(a1_ref, a2_ref, w1_ref, w2_ref, r_ref, o_ref):
    o_ref[...] = r_ref[...] + (_dot(a1_ref[...], w1_ref[...]) + _dot(a2_ref[...], w2_ref[...]))


def _mm_swiglu_body(a_ref, wg_ref, wu_ref, o_ref):
    a = a_ref[...]
    g = _dot(a, wg_ref[...])
    u = _dot(a, wu_ref[...])
    o_ref[...] = (g * jax.nn.sigmoid(g) * u).astype(o_ref.dtype)


def _matmul(a, w, *, out_dtype, tm, tn, a_col_block=0, name="matmul"):
    m = a.shape[0]
    k, n = w.shape
    return pl.pallas_call(
        _mm_plain_body,
        grid=(n // tn, m // tm),
        in_specs=[pl.BlockSpec((tm, k), lambda j, i: (i, a_col_block)),
                  pl.BlockSpec((k, tn), lambda j, i: (0, j))],
        out_specs=pl.BlockSpec((tm, tn), lambda j, i: (i, j)),
        out_shape=jax.ShapeDtypeStruct((m, n), out_dtype),
        compiler_params=_params("parallel", "parallel"),
        name=name,
    )(a, w)


def _matmul_mul(a, w, table, *, out_dtype, tm, name="matmul_mul"):
    m, k = a.shape
    n = w.shape[1]
    tn = table.shape[1]
    return pl.pallas_call(
        _mm_mul_body,
        grid=(n // tn, m // tm),
        in_specs=[pl.BlockSpec((tm, k), lambda j, i: (i, 0)),
                  pl.BlockSpec((k, tn), lambda j, i: (0, j)),
                  pl.BlockSpec((tm, tn), lambda j, i: (i, 0))],
        out_specs=pl.BlockSpec((tm, tn), lambda j, i: (i, j)),
        out_shape=jax.ShapeDtypeStruct((m, n), out_dtype),
        compiler_params=_params("parallel", "parallel"),
        name=name,
    )(a, w, table)


def _matmul_res(a, w, res, *, tm, tn, name="matmul_res"):
    m, k = a.shape
    n = w.shape[1]
    return pl.pallas_call(
        _mm_res_body,
        grid=(n // tn, m // tm),
        in_specs=[pl.BlockSpec((tm, k), lambda j, i: (i, 0)),
                  pl.BlockSpec((k, tn), lambda j, i: (0, j)),
                  pl.BlockSpec((tm, tn), lambda j, i: (i, j))],
        out_specs=pl.BlockSpec((tm, tn), lambda j, i: (i, j)),
        out_shape=jax.ShapeDtypeStruct((m, n), F32),
        compiler_params=_params("parallel", "parallel"),
        name=name,
    )(a, w, res)


def _matmul_res2(a1, a2, w, res, *, tm, tn, name="matmul_res2"):
    m, k1 = a1.shape
    k2 = a2.shape[1]
    n = w.shape[1]
    assert k1 % k2 == 0
    return pl.pallas_call(
        _mm_res2_body,
        grid=(n // tn, m // tm),
        in_specs=[pl.BlockSpec((tm, k1), lambda j, i: (i, 0)),
                  pl.BlockSpec((tm, k2), lambda j, i: (i, 0)),
                  pl.BlockSpec((k1, tn), lambda j, i: (0, j)),
                  pl.BlockSpec((k2, tn), lambda j, i: (k1 // k2, j)),
                  pl.BlockSpec((tm, tn), lambda j, i: (i, j))],
        out_specs=pl.BlockSpec((tm, tn), lambda j, i: (i, j)),
        out_shape=jax.ShapeDtypeStruct((m, n), F32),
        compiler_params=_params("parallel", "parallel"),
        name=name,
    )(a1, a2, w, w, res)


def _matmul_swiglu(a, w_gu, *, tm, tn, name="matmul_swiglu"):
    m, k = a.shape
    f = w_gu.shape[1] // 2
    nb = f // tn
    return pl.pallas_call(
        _mm_swiglu_body,
        grid=(nb, m // tm),
        in_specs=[pl.BlockSpec((tm, k), lambda j, i: (i, 0)),
                  pl.BlockSpec((k, tn), lambda j, i: (0, j)),
                  pl.BlockSpec((k, tn), lambda j, i: (0, j + nb))],
        out_specs=pl.BlockSpec((tm, tn), lambda j, i: (i, j)),
        out_shape=jax.ShapeDtypeStruct((m, f), BF16),
        compiler_params=_params("parallel", "parallel"),
        name=name,
    )(a, w_gu, w_gu)


def _krope_body(x_ref, t_ref, o_ref):
    y = x_ref[...] * t_ref[...]
    o_ref[...] = (y + pltpu.roll(y, MLA_ROPE_DIM, 1)).astype(o_ref.dtype)


def _krope(p, table, *, col_block, name="mla_krope"):
    t = p.shape[0]
    tm = _pick(t, (1024, 512, 256, 128, 8))
    w = 2 * MLA_ROPE_DIM
    return pl.pallas_call(
        _krope_body,
        grid=(t // tm,),
        in_specs=[pl.BlockSpec((tm, w), lambda i: (i, col_block)),
                  pl.BlockSpec((tm, w), lambda i: (i, 0))],
        out_specs=pl.BlockSpec((tm, w), lambda i: (i, 0)),
        out_shape=jax.ShapeDtypeStruct((t, w), BF16),
        compiler_params=_params("parallel"),
        name=name,
    )(p, table)


def _flash_body(q_ref, kn_ref, v_ref, kr_ref, o_ref, k_scr, *, tk):
    @pl.when(pl.program_id(2) == 0)
    def _():
        k_scr[:, :MLA_NOPE_DIM] = kn_ref[...]
        k_scr[:, MLA_NOPE_DIM:] = kr_ref[...]

    q = q_ref[...]
    tq = q.shape[0]
    s_len = k_scr.shape[0]
    m = jnp.full((tq, 1), -jnp.inf, F32)
    l = jnp.zeros((tq, 1), F32)
    acc = jnp.zeros((tq, MLA_V_DIM), F32)
    for c in range(s_len // tk):
        s = _dot_nt(q, k_scr[c * tk:(c + 1) * tk, :])
        m_new = jnp.maximum(m, jnp.max(s, axis=-1, keepdims=True))
        alpha = jnp.exp(m - m_new)
        p = jnp.exp(s - m_new)
        l = alpha * l + jnp.sum(p, axis=-1, keepdims=True)
        acc = alpha * acc + _dot(p.astype(BF16), v_ref[c * tk:(c + 1) * tk, :])
        m = m_new
    o_ref[...] = (acc / l).astype(o_ref.dtype)


def _flash_attention(q, kv, kr, *, batch, heads, name="mla_flash"):
    t = q.shape[0]
    s_len = t // batch
    tq = _pick(s_len, (512, 256, 128))
    tk = _pick(s_len, (1024, 512, 256, 128))
    nq = s_len // tq
    return pl.pallas_call(
        functools.partial(_flash_body, tk=tk),
        grid=(batch, heads, nq),
        in_specs=[pl.BlockSpec((tq, MLA_HEAD_COLS), lambda b, h, i: (b * nq + i, h)),
                  pl.BlockSpec((s_len, MLA_NOPE_DIM), lambda b, h, i: (b, 2 * h)),
                  pl.BlockSpec((s_len, MLA_V_DIM), lambda b, h, i: (b, 2 * h + 1)),
                  pl.BlockSpec((s_len, 2 * MLA_ROPE_DIM), lambda b, h, i: (b, 0))],
        out_specs=pl.BlockSpec((tq, MLA_V_DIM), lambda b, h, i: (b * nq + i, h)),
        out_shape=jax.ShapeDtypeStruct((t, heads * MLA_V_DIM), BF16),
        scratch_shapes=[pltpu.VMEM((s_len, MLA_HEAD_COLS), BF16)],
        compiler_params=_params("parallel", "parallel", "arbitrary"),
        name=name,
    )(q, kv, kv, kr)


def _memattn_body(*refs, scale):
    q_refs = refs[:MEM_HEADS]
    mk_ref, mv_ref, o_ref = refs[MEM_HEADS:]
    dh = q_refs[0].shape[1]
    for h in range(MEM_HEADS):
        q = (q_refs[h][...].astype(F32) * scale).astype(BF16)
        s = _dot_nt(q, mk_ref[:, h * dh:(h + 1) * dh])
        p = jnp.exp(s - jnp.max(s, axis=-1, keepdims=True))
        l = jnp.sum(p, axis=-1, keepdims=True)
        o = _dot((p / l).astype(BF16), mv_ref[:, h * dh:(h + 1) * dh])
        o_ref[:, h * dh:(h + 1) * dh] = o.astype(o_ref.dtype)


def _memory_attention(p, mkv, *, batch, q_col, width, name="mem_attn"):
    t = p.shape[0]
    s_len = t // batch
    n_mem = mkv.shape[0] // batch
    dh = width // MEM_HEADS
    tq = _pick(s_len, (512, 256, 128))
    nq = s_len // tq
    q_specs = [pl.BlockSpec((tq, dh), functools.partial(lambda b, i, h: (b * nq + i, q_col // dh + h), h=h))
               for h in range(MEM_HEADS)]
    return pl.pallas_call(
        functools.partial(_memattn_body, scale=dh ** -0.5),
        grid=(batch, nq),
        in_specs=q_specs + [pl.BlockSpec((n_mem, width), lambda b, i: (b, 0)),
                            pl.BlockSpec((n_mem, width), lambda b, i: (b, 1))],
        out_specs=pl.BlockSpec((tq, width), lambda b, i: (b * nq + i, 0)),
        out_shape=jax.ShapeDtypeStruct((t, width), BF16),
        compiler_params=_params("parallel", "parallel"),
        name=name,
    )(*([p] * MEM_HEADS), mkv, mkv)


def _gates_body(w_ref, x_ref, b_ref, f_ref, o_ref):
    g = _dot_nt(w_ref[...], x_ref[...]) + b_ref[...]
    igate = IGATE_CAP * jnp.tanh(g / IGATE_CAP)
    log_fgate = jnp.minimum(g, 0.0) - jnp.log1p(jnp.exp(-jnp.abs(g)))
    o_ref[...] = jnp.where(f_ref[...] > 0.5, log_fgate, igate)


def _mlstm_gates(w_t, hn, bias, *, heads, name="mlstm_gates"):
    rows, k = w_t.shape
    t = hn.shape[0]
    tn = _pick(t, (1024, 512, 256, 128))
    is_f = (jnp.arange(rows) // heads) % 2 == 1
    return pl.pallas_call(
        _gates_body,
        grid=(t // tn,),
        in_specs=[pl.BlockSpec((rows, k), lambda i: (0, 0)),
                  pl.BlockSpec((tn, k), lambda i: (i, 0)),
                  pl.BlockSpec((rows, 1), lambda i: (0, 0)),
                  pl.BlockSpec((rows, 1), lambda i: (0, 0))],
        out_specs=pl.BlockSpec((rows, tn), lambda i: (0, i)),
        out_shape=jax.ShapeDtypeStruct((rows, t), F32),
        compiler_params=_params("parallel"),
        name=name,
    )(w_t, hn, bias.reshape(rows, 1).astype(F32), is_f.astype(F32).reshape(rows, 1))


def _conv_body(x_ref, prev_ref, next_ref, w_ref, o_ref, *, blocks_per_seq, q_blocks, q_scale):
    i = pl.program_id(0)
    j = pl.program_id(1)
    x = x_ref[...]
    ts = x.shape[0]
    pos = i % blocks_per_seq
    prev_row = jnp.where(pos == 0, 0.0, prev_ref[7:8, :])
    next_row = jnp.where(pos == blocks_per_seq - 1, 0.0, next_ref[0:1, :])
    row = lax.broadcasted_iota(jnp.int32, x.shape, 0)
    x_prev = jnp.where(row == 0, prev_row, pltpu.roll(x, 1, 0))
    x_next = jnp.where(row == ts - 1, next_row, pltpu.roll(x, ts - 1, 0))
    y = x_prev * w_ref[0:1, :] + x * w_ref[1:2, :] + x_next * w_ref[2:3, :]
    y = y * jax.nn.sigmoid(y)
    scale = jnp.where(j < q_blocks, q_scale, 1.0)
    o_ref[...] = (y * scale).astype(o_ref.dtype)


def _mlstm_conv(p, conv_w, *, batch, width, q_width, name="mlstm_conv"):
    t = p.shape[0]
    s_len = t // batch
    ts = _pick(s_len, (512, 256, 128))
    tc = _pick(q_width, (512, 256, 128))
    hb = ts // 8
    return pl.pallas_call(
        functools.partial(_conv_body, blocks_per_seq=s_len // ts, q_blocks=q_width // tc,
                          q_scale=MLSTM_QK_DIM ** -0.5),
        grid=(t // ts, width // tc),
        in_specs=[pl.BlockSpec((ts, tc), lambda i, j: (i, j)),
                  pl.BlockSpec((8, tc), lambda i, j: (jnp.maximum(i * hb - 1, 0), j)),
                  pl.BlockSpec((8, tc), lambda i, j: (jnp.minimum((i + 1) * hb, t // 8 - 1), j)),
                  pl.BlockSpec((3, tc), lambda i, j: (0, j))],
        out_specs=pl.BlockSpec((ts, tc), lambda i, j: (i, j)),
        out_shape=jax.ShapeDtypeStruct((t, width), BF16),
        compiler_params=_params("parallel", "parallel"),
        name=name,
    )(p, p, p, conv_w.astype(F32))


def _mlstm_scan_body(q_ref, k_ref, v_ref, ig_ref, lf_ref, o_ref, c_scr, n_scr, m_scr):
    d = pl.program_id(1)

    @pl.when(pl.program_id(2) == 0)
    def _():
        c_scr[...] = jnp.zeros_like(c_scr)
        n_scr[...] = jnp.zeros_like(n_scr)
        m_scr[...] = jnp.full_like(m_scr, STAB_INIT)

    q = q_ref[...]
    k = k_ref[...]
    v = v_ref[...].astype(BF16)
    ig = ig_ref[0]
    lf = lf_ref[0]
    ln = q.shape[0]
    t_idx = lax.broadcasted_iota(jnp.int32, (ln, ln), 0)
    j_idx = lax.broadcasted_iota(jnp.int32, (ln, ln), 1)
    sign = 1 - 2 * d
    mask = (j_idx - t_idx) * sign <= 0
    eye = j_idx == t_idx
    lf_b = jnp.broadcast_to(lf, (ln, ln))
    ig_b = jnp.broadcast_to(ig, (ln, ln))
    cs_col = jnp.sum(jnp.where(mask, lf_b, 0.0), axis=1, keepdims=True)
    cs_row = jnp.sum(jnp.where(eye, cs_col, 0.0), axis=0, keepdims=True)
    ig_col = jnp.sum(jnp.where(eye, ig_b, 0.0), axis=1, keepdims=True)
    m_prev = m_scr[...]

    dmat = jnp.where(mask, cs_col - cs_row + ig, -jnp.inf)
    inter = cs_col + m_prev
    m_row = jnp.maximum(jnp.max(dmat, axis=1, keepdims=True), inter)
    s = _dot_nt(q, k) * jnp.exp(dmat - m_row)
    w_inter = jnp.exp(inter - m_row)
    num = _dot(s.astype(BF16), v) + w_inter * _dot(q, c_scr[...].astype(BF16))
    qn = jnp.sum(q.astype(F32) * n_scr[...], axis=1, keepdims=True)
    den = jnp.sum(s, axis=1, keepdims=True) + w_inter * qn
    o_ref[...] = num / jnp.maximum(jnp.abs(den), jnp.exp(-m_row))

    b_end = jnp.sum(lf, axis=1, keepdims=True)
    w_tok_row = b_end - cs_row + ig
    w_tok_col = b_end - cs_col + ig_col
    m_new = jnp.maximum(b_end + m_prev, jnp.max(w_tok_row, axis=1, keepdims=True))
    decay = jnp.exp(b_end + m_prev - m_new)
    wk = k.astype(F32) * jnp.exp(w_tok_col - m_new)
    c_scr[...] = decay * c_scr[...] + _dot_tn(wk.astype(BF16), v)
    n_scr[...] = decay * n_scr[...] + jnp.sum(wk, axis=0, keepdims=True)
    m_scr[...] = m_new


def _mlstm_scan(qk, p, gates, *, batch, heads, v_col, name="mlstm_scan"):
    t = qk.shape[0]
    s_len = t // batch
    ln = MLSTM_CHUNK
    nc = s_len // ln
    dk, dv = MLSTM_QK_DIM, MLSTM_V_DIM

    def chunk(d, c):
        return c + d * (nc - 1 - 2 * c)

    def row_block(bh, d, c):
        return (bh // heads) * nc + chunk(d, c)

    def gate_block(kind):
        return lambda bh, d, c: ((((2 * d + kind) * heads + bh % heads) * batch + bh // heads) * nc + chunk(d, c), 0, 0)

    return pl.pallas_call(
        _mlstm_scan_body,
        grid=(batch * heads, 2, nc),
        in_specs=[pl.BlockSpec((ln, dk), lambda bh, d, c: (row_block(bh, d, c), bh % heads)),
                  pl.BlockSpec((ln, dk), lambda bh, d, c: (row_block(bh, d, c), heads + bh % heads)),
                  pl.BlockSpec((ln, dv), lambda bh, d, c: (row_block(bh, d, c), v_col // dv + bh % heads)),
                  pl.BlockSpec((1, 1, ln), gate_block(0)),
                  pl.BlockSpec((1, 1, ln), gate_block(1))],
        out_specs=pl.BlockSpec((None, ln, dv), lambda bh, d, c: (d, row_block(bh, d, c), bh % heads)),
        out_shape=jax.ShapeDtypeStruct((2, t, heads * dv), F32),
        scratch_shapes=[pltpu.VMEM((dk, dv), F32), pltpu.VMEM((1, dk), F32), pltpu.VMEM((1, 1), F32)],
        compiler_params=_params("parallel", "parallel", "arbitrary"),
        name=name,
    )(qk, qk, p, gates, gates)


def _combine_body(hf_ref, hb_ref, o_pre_ref, g_ref, out_ref):
    h = hf_ref[...] + hb_ref[...]
    ms = jnp.mean(h * h, axis=-1, keepdims=True)
    y = h * lax.rsqrt(ms + EPS) * g_ref[...]
    out_ref[...] = (y * jax.nn.sigmoid(o_pre_ref[...])).astype(out_ref.dtype)


def _mlstm_combine(h_dirs, p, head_norm, *, heads, o_col, name="mlstm_combine"):
    t = p.shape[0]
    dv = MLSTM_V_DIM
    ts = _pick(t, (512, 256, 128))
    return pl.pallas_call(
        _combine_body,
        grid=(t // ts, heads),
        in_specs=[pl.BlockSpec((None, ts, dv), lambda i, h: (0, i, h)),
                  pl.BlockSpec((None, ts, dv), lambda i, h: (1, i, h)),
                  pl.BlockSpec((ts, dv), lambda i, h: (i, o_col // dv + h)),
                  pl.BlockSpec((1, dv), lambda i, h: (0, h))],
        out_specs=pl.BlockSpec((ts, dv), lambda i, h: (i, h)),
        out_shape=jax.ShapeDtypeStruct((t, heads * dv), BF16),
        compiler_params=_params("parallel", "parallel"),
        name=name,
    )(h_dirs, h_dirs, p, head_norm.reshape(1, heads * dv).astype(F32))


def _swap_halves(w):
    half = w.shape[-1] // 2
    return jnp.concatenate([w[..., half:], w[..., :half]], axis=-1)


def _prep_mla_weights(w_in, w_uq, q_rank, kv_rank, heads):
    a1 = q_rank + kv_rank
    a2 = a1 + MLA_ROPE_DIM
    w_kr = w_in[:, a1:a2]
    w_in_p = jnp.concatenate([w_in[:, :a1], w_in[:, a2:], w_kr, _swap_halves(w_kr)], axis=1).astype(BF16)
    wq = w_uq.reshape(q_rank, heads, MLA_NOPE_DIM + MLA_ROPE_DIM)
    rope_cols = wq[..., MLA_NOPE_DIM:]
    w_uq_p = jnp.concatenate([wq[..., :MLA_NOPE_DIM], rope_cols, _swap_halves(rope_cols)], axis=-1)
    return w_in_p, w_uq_p.reshape(q_rank, heads * MLA_HEAD_COLS).astype(BF16)


def _rope_tables(positions):
    inv_freq = 1.0 / (ROPE_THETA ** (jnp.arange(0, MLA_ROPE_DIM, 2, dtype=F32) / MLA_ROPE_DIM))
    ang = positions.astype(F32).reshape(-1, 1) * inv_freq
    cos, sin = jnp.cos(ang), jnp.sin(ang)
    k_tab = jnp.concatenate([cos, cos, -sin, sin], axis=1)
    scale = (MLA_NOPE_DIM + MLA_ROPE_DIM) ** -0.5
    q_tab = jnp.concatenate([jnp.ones((ang.shape[0], MLA_NOPE_DIM), F32), k_tab], axis=1) * scale
    return q_tab, k_tab


def kernel(x, positions, mem, attn_norm, ffn_norm, mem_norm, final_norm, mla_w_in, mla_q_norm, mla_kv_norm,
           mla_w_uq, mla_w_ukv, mlstm_w_in, mlstm_conv_w, mlstm_gate_b, mlstm_head_norm, w_mem_kv, w_out,
           w_gu, w_down):
    batch, s_len, d_model = x.shape
    depth = attn_norm.shape[0]
    n_mem = mem.shape[1]
    t = batch * s_len
    mem_width = d_model // 4
    main_width = d_model - mem_width
    mla_heads = main_width // MLA_V_DIM
    mlstm_heads = main_width // MLSTM_V_DIM
    qk_width = mlstm_heads * MLSTM_QK_DIM
    v_width = mlstm_heads * MLSTM_V_DIM
    q_rank = mla_q_norm.shape[1]
    kv_rank = mla_kv_norm.shape[1]
    d_ff = w_down.shape[1]

    tm = _pick(t, (1024, 512, 256, 128))
    q_tab, k_tab = _rope_tables(positions)
    mem_n = _rmsnorm(mem.reshape(batch * n_mem, d_model), mem_norm, width=d_model, name="mem_norm")
    h = x.reshape(t, d_model)

    for i in range(depth):
        j = i // N_MIXERS
        hn = _rmsnorm(h, attn_norm[i], width=d_model, name="attn_norm")
        mkv = _matmul(mem_n, w_mem_kv[i].astype(BF16), out_dtype=BF16, tm=_pick(batch * n_mem, (512, 256, 128)),
                      tn=_pick(2 * mem_width, (1024, 512, 256)), name="mem_kv")
        if i % N_MIXERS == 0:
            w_in_p, w_uq_p = _prep_mla_weights(mla_w_in[j], mla_w_uq[j], q_rank, kv_rank, mla_heads)
            n_in = w_in_p.shape[1]
            p = _matmul(hn, w_in_p, out_dtype=F32, tm=tm, tn=_pick(n_in, (896, 768, 640, 512, 384, 256, 128)),
                        name="mla_in_proj")
            cq = _rmsnorm(p, mla_q_norm[j], width=q_rank, col_block=0, name="mla_q_norm")
            ckv = _rmsnorm(p, mla_kv_norm[j], width=kv_rank, col_block=q_rank // kv_rank, name="mla_kv_norm")
            q = _matmul_mul(cq, w_uq_p, q_tab, out_dtype=BF16, tm=tm, name="mla_q_proj")
            kv = _matmul(ckv, mla_w_ukv[j].astype(BF16), out_dtype=BF16, tm=tm,
                         tn=_pick(mla_heads * MLA_HEAD_COLS, (1024, 768, 512, 256)), name="mla_kv_proj")
            q_mem_col = q_rank + kv_rank
            kr = _krope(p, k_tab, col_block=(q_mem_col + mem_width) // (2 * MLA_ROPE_DIM))
            main = _flash_attention(q, kv, kr, batch=batch, heads=mla_heads)
        else:
            w_in = mlstm_w_in[j]
            g_col = 2 * qk_width + 2 * v_width
            w_in_p = jnp.concatenate([w_in[:, :g_col], w_in[:, g_col + 4 * mlstm_heads:]], axis=1).astype(BF16)
            w_gates_t = w_in[:, g_col:g_col + 4 * mlstm_heads].T.astype(BF16)
            p = _matmul(hn, w_in_p, out_dtype=F32, tm=tm, tn=_pick(w_in_p.shape[1], (1024, 512, 256, 128)),
                        name="mlstm_in_proj")
            gates = _mlstm_gates(w_gates_t, hn, mlstm_gate_b[j], heads=mlstm_heads)
            gates = gates.reshape(-1, 1, MLSTM_CHUNK)
            qk = _mlstm_conv(p, mlstm_conv_w[j], batch=batch, width=2 * qk_width, q_width=qk_width)
            h_dirs = _mlstm_scan(qk, p, gates, batch=batch, heads=mlstm_heads, v_col=2 * qk_width)
            main = _mlstm_combine(h_dirs, p, mlstm_head_norm[j], heads=mlstm_heads, o_col=2 * qk_width + v_width)
            q_mem_col = g_col
        mem_out = _memory_attention(p, mkv, batch=batch, q_col=q_mem_col, width=mem_width)
        h = _matmul_res2(main, mem_out, w_out[i].astype(BF16), h, tm=tm, tn=_pick(d_model, (512, 256)),
                         name="out_proj")
        hn = _rmsnorm(h, ffn_norm[i], width=d_model, name="ffn_norm")
        act = _matmul_swiglu(hn, w_gu[i].astype(BF16), tm=tm, tn=_pick(d_ff, (512, 256, 128)), name="ffn_up")
        h = _matmul_res(act, w_down[i].astype(BF16), h, tm=_pick(t, (512, 256, 128)),
                        tn=_pick(d_model, (512, 256)), name="ffn_down")
    out = _rmsnorm(h, final_norm, width=d_model, out_dtype=x.dtype, name="final_norm")
    return out.reshape(batch, s_len, d_model)
```

```python
import functools
import math

import jax
import jax.numpy as jnp
from jax import lax
from jax.experimental import pallas as pl
from jax.experimental.pallas import tpu as pltpu

F32 = jnp.float32
BF16 = jnp.bfloat16

EPS = 1e-6
ROPE_THETA = 10000.0
MLA_NOPE_DIM = 128
MLA_ROPE_DIM = 64
MLA_V_DIM = 128
MLA_HEAD_COLS = 2 * MLA_NOPE_DIM
MLSTM_V_DIM = 512
MLSTM_QK_DIM = MLSTM_V_DIM // 2
MLSTM_CHUNK = 128
MEM_HEADS = 4
IGATE_CAP = 15.0
STAB_INIT = -1e30
N_MIXERS = 2

V7X_VMEM_BYTES = 64 * 1024 * 1024
VMEM_LIMIT_BYTES = V7X_VMEM_BYTES - 8 * 1024 * 1024


def _params(*semantics):
    return pltpu.CompilerParams(dimension_semantics=semantics, vmem_limit_bytes=VMEM_LIMIT_BYTES)


def _pick(n, candidates):
    for c in candidates:
        if n % c == 0:
            return c
    return n


def _dot(a, b):
    return jnp.dot(a, b, preferred_element_type=F32)


def _dot_nt(a, b):
    return lax.dot_general(a, b, (((1,), (1,)), ((), ())), preferred_element_type=F32)


def _dot_tn(a, b):
    return lax.dot_general(a, b, (((0,), (0,)), ((), ())), preferred_element_type=F32)


def _rmsnorm_body(x_ref, g_ref, o_ref):
    x = x_ref[...].astype(F32)
    ms = jnp.mean(x * x, axis=-1, keepdims=True)
    o_ref[...] = (x * lax.rsqrt(ms + EPS) * g_ref[...]).astype(o_ref.dtype)


def _rmsnorm(x, g, *, width, col_block=0, out_dtype=BF16, name="rmsnorm"):
    t = x.shape[0]
    tm = _pick(t, (256, 128, 64, 8))
    return pl.pallas_call(
        _rmsnorm_body,
        grid=(t // tm,),
        in_specs=[pl.BlockSpec((tm, width), lambda i: (i, col_block)),
                  pl.BlockSpec((1, width), lambda i: (0, 0))],
        out_specs=pl.BlockSpec((tm, width), lambda i: (i, 0)),
        out_shape=jax.ShapeDtypeStruct((t, width), out_dtype),
        compiler_params=_params("parallel"),
        name=name,
    )(x, g.reshape(1, width).astype(F32))


def _ep_plain(a, w, extra, o_ref):
    o_ref[...] = _dot(a[0][...].astype(BF16), w[0][...]).astype(o_ref.dtype)


def _ep_mul(a, w, extra, o_ref):
    table = extra[0][...]
    reps = o_ref.shape[1] // table.shape[1]
    if reps > 1:
        table = jnp.concatenate([table] * reps, axis=1)
    o_ref[...] = (_dot(a[0][...].astype(BF16), w[0][...]) * table).astype(o_ref.dtype)


def _ep_residual(a, w, extra, o_ref):
    acc = _dot(a[0][...], w[0][...])
    for a_ref, w_ref in zip(a[1:], w[1:]):
        acc = acc + _dot(a_ref[...], w_ref[...])
    o_ref[...] = extra[0][...] + acc


def _ep_swiglu(a, w, extra, o_ref):
    x = a[0][...]
    g = _dot(x, w[0][...])
    u = _dot(x, w[1][...])
    o_ref[...] = (g * jax.nn.sigmoid(g) * u).astype(o_ref.dtype)


def _mm_body(*refs, n_a, n_w, n_extra, epilogue, cast):
    a = refs[:n_a]
    w = refs[n_a:n_a + n_w]
    extra = refs[n_a + n_w:n_a + n_w + n_extra]
    o_ref = refs[n_a + n_w + n_extra]
    if cast:
        scratch = refs[n_a + n_w + n_extra + 1:]

        @pl.when(pl.program_id(1) == 0)
        def _():
            for w_ref, s_ref in zip(w, scratch):
                s_ref[...] = w_ref[...].astype(BF16)

        w = scratch
    epilogue(a, w, extra, o_ref)


def _mm(name, epilogue, a_ops, w_ops, extra_ops, *, m, n, tm, tn, out_dtype):
    a_specs = [pl.BlockSpec(bs, im) for _, bs, im in a_ops]
    e_specs = [pl.BlockSpec(bs, im) for _, bs, im in extra_ops]
    w_specs = [pl.BlockSpec((None, rows, tn), functools.partial(
        lambda j, i, layer, rb, co: (layer, rb, j + co), layer=layer, rb=rb, co=co))
        for _, layer, rows, rb, co in w_ops]
    cast = w_ops[0][0].dtype != BF16
    scratch = [pltpu.VMEM((rows, tn), BF16) for _, _, rows, _, _ in w_ops] if cast else []
    return pl.pallas_call(
        functools.partial(_mm_body, n_a=len(a_ops), n_w=len(w_ops), n_extra=len(extra_ops),
                          epilogue=epilogue, cast=cast),
        grid=(n // tn, m // tm),
        in_specs=a_specs + w_specs + e_specs,
        out_specs=pl.BlockSpec((tm, tn), lambda j, i: (i, j)),
        out_shape=jax.ShapeDtypeStruct((m, n), out_dtype),
        scratch_shapes=scratch,
        compiler_params=_params("parallel", "arbitrary"),
        name=name,
    )(*[op[0] for op in a_ops], *[op[0] for op in w_ops], *[op[0] for op in extra_ops])


def _rows(tm, k, col_block=0):
    return (tm, k), lambda j, i: (i, col_block)


def _as3d(w):
    return w if w.ndim == 3 else w[None]


def _matmul(a, w, *, layer=0, n=None, out_dtype, tm, tn, name):
    w = _as3d(w)
    m, k = a.shape
    n = w.shape[2] if n is None else n
    return _mm(name, _ep_plain, [(a, *_rows(tm, k))], [(w, layer, k, 0, 0)], [],
               m=m, n=n, tm=tm, tn=tn, out_dtype=out_dtype)


def _matmul_mul(a, w, table, *, out_dtype, tm, tn, name):
    w = _as3d(w)
    m, k = a.shape
    c = table.shape[1]
    return _mm(name, _ep_mul, [(a, *_rows(tm, k))], [(w, 0, k, 0, 0)], [(table, (tm, c), lambda j, i: (i, 0))],
               m=m, n=w.shape[2], tm=tm, tn=tn, out_dtype=out_dtype)


def _matmul_res(a, w, res, *, layer, tm, tn, name):
    m, k = a.shape
    return _mm(name, _ep_residual, [(a, *_rows(tm, k))], [(w, layer, k, 0, 0)],
               [(res, (tm, tn), lambda j, i: (i, j))], m=m, n=w.shape[2], tm=tm, tn=tn, out_dtype=F32)


def _matmul_res2(a1, a2, w, res, *, layer, tm, tn, name):
    m, k1 = a1.shape
    k2 = a2.shape[1]
    assert k1 % k2 == 0
    return _mm(name, _ep_residual, [(a1, *_rows(tm, k1)), (a2, *_rows(tm, k2))],
               [(w, layer, k1, 0, 0), (w, layer, k2, k1 // k2, 0)],
               [(res, (tm, tn), lambda j, i: (i, j))], m=m, n=w.shape[2], tm=tm, tn=tn, out_dtype=F32)


def _matmul_swiglu(a, w_gu, *, layer, tm, tn, name):
    m, k = a.shape
    f = w_gu.shape[2] // 2
    return _mm(name, _ep_swiglu, [(a, *_rows(tm, k))], [(w_gu, layer, k, 0, 0), (w_gu, layer, k, 0, f // tn)], [],
               m=m, n=f, tm=tm, tn=tn, out_dtype=BF16)


def _krope_body(x_ref, t_ref, o_ref):
    y = x_ref[...] * t_ref[...]
    o_ref[...] = (y + pltpu.roll(y, MLA_ROPE_DIM, 1)).astype(o_ref.dtype)


def _krope(p, table, *, col_block, name="mla_krope"):
    t = p.shape[0]
    tm = _pick(t, (1024, 512, 256, 128, 8))
    w = 2 * MLA_ROPE_DIM
    return pl.pallas_call(
        _krope_body,
        grid=(t // tm,),
        in_specs=[pl.BlockSpec((tm, w), lambda i: (i, col_block)),
                  pl.BlockSpec((tm, w), lambda i: (i, 0))],
        out_specs=pl.BlockSpec((tm, w), lambda i: (i, 0)),
        out_shape=jax.ShapeDtypeStruct((t, w), BF16),
        compiler_params=_params("parallel"),
        name=name,
    )(p, table)


def _flash_body(q_ref, kn_ref, v_ref, kr_ref, o_ref, k_scr, *, tk):
    @pl.when(pl.program_id(2) == 0)
    def _():
        k_scr[:, :MLA_NOPE_DIM] = kn_ref[...]
        k_scr[:, MLA_NOPE_DIM:] = kr_ref[...]

    q = q_ref[...]
    tq = q.shape[0]
    s_len = k_scr.shape[0]
    m = jnp.full((tq, 1), -jnp.inf, F32)
    l = jnp.zeros((tq, 1), F32)
    acc = jnp.zeros((tq, MLA_V_DIM), F32)
    for c in range(s_len // tk):
        s = _dot_nt(q, k_scr[c * tk:(c + 1) * tk, :])
        m_new = jnp.maximum(m, jnp.max(s, axis=-1, keepdims=True))
        alpha = jnp.exp2(m - m_new)
        p = jnp.exp2(s - m_new)
        l = alpha * l + jnp.sum(p, axis=-1, keepdims=True)
        acc = alpha * acc + _dot(p.astype(BF16), v_ref[c * tk:(c + 1) * tk, :])
        m = m_new
    o_ref[...] = (acc / l).astype(o_ref.dtype)


def _flash_attention(q, kv, kr, *, batch, heads, name="mla_flash"):
    t = q.shape[0]
    s_len = t // batch
    tq = _pick(s_len, (512, 256, 128))
    tk = _pick(s_len, (1024, 512, 256, 128))
    nq = s_len // tq
    return pl.pallas_call(
        functools.partial(_flash_body, tk=tk),
        grid=(batch, heads, nq),
        in_specs=[pl.BlockSpec((tq, MLA_HEAD_COLS), lambda b, h, i: (b * nq + i, h)),
                  pl.BlockSpec((s_len, MLA_NOPE_DIM), lambda b, h, i: (b, 2 * h)),
                  pl.BlockSpec((s_len, MLA_V_DIM), lambda b, h, i: (b, 2 * h + 1)),
                  pl.BlockSpec((s_len, 2 * MLA_ROPE_DIM), lambda b, h, i: (b, 0))],
        out_specs=pl.BlockSpec((tq, MLA_V_DIM), lambda b, h, i: (b * nq + i, h)),
        out_shape=jax.ShapeDtypeStruct((t, heads * MLA_V_DIM), BF16),
        scratch_shapes=[pltpu.VMEM((s_len, MLA_HEAD_COLS), BF16)],
        compiler_params=_params("parallel", "parallel", "arbitrary"),
        name=name,
    )(q, kv, kv, kr)


def _memattn_body(*refs, scale):
    q_refs = refs[:MEM_HEADS]
    mk_ref, mv_ref, o_ref = refs[MEM_HEADS:]
    dh = q_refs[0].shape[1]
    for h in range(MEM_HEADS):
        q = (q_refs[h][...].astype(F32) * scale).astype(BF16)
        s = _dot_nt(q, mk_ref[:, h * dh:(h + 1) * dh])
        p = jnp.exp(s - jnp.max(s, axis=-1, keepdims=True))
        l = jnp.sum(p, axis=-1, keepdims=True)
        o = _dot((p / l).astype(BF16), mv_ref[:, h * dh:(h + 1) * dh])
        o_ref[:, h * dh:(h + 1) * dh] = o.astype(o_ref.dtype)


def _memory_attention(p, mkv, *, batch, q_col, width, name="mem_attn"):
    t = p.shape[0]
    s_len = t // batch
    n_mem = mkv.shape[0] // batch
    dh = width // MEM_HEADS
    tq = _pick(s_len, (512, 256, 128))
    nq = s_len // tq
    q_specs = [pl.BlockSpec((tq, dh), functools.partial(lambda b, i, h: (b * nq + i, q_col // dh + h), h=h))
               for h in range(MEM_HEADS)]
    return pl.pallas_call(
        functools.partial(_memattn_body, scale=dh ** -0.5),
        grid=(batch, nq),
        in_specs=q_specs + [pl.BlockSpec((n_mem, width), lambda b, i: (b, 0)),
                            pl.BlockSpec((n_mem, width), lambda b, i: (b, 1))],
        out_specs=pl.BlockSpec((tq, width), lambda b, i: (b * nq + i, 0)),
        out_shape=jax.ShapeDtypeStruct((t, width), BF16),
        compiler_params=_params("parallel", "parallel"),
        name=name,
    )(*([p] * MEM_HEADS), mkv, mkv)


def _gates_body(w_ref, x_ref, b_ref, f_ref, o_ref):
    g = _dot_nt(w_ref[...], x_ref[...]) + b_ref[...]
    igate = IGATE_CAP * jnp.tanh(g / IGATE_CAP)
    log_fgate = jnp.minimum(g, 0.0) - jnp.log1p(jnp.exp(-jnp.abs(g)))
    o_ref[...] = jnp.where(f_ref[...] > 0.5, log_fgate, igate)


def _mlstm_gates(w_t, hn, bias, *, heads, name="mlstm_gates"):
    rows, k = w_t.shape
    t = hn.shape[0]
    tn = _pick(t, (1024, 512, 256, 128))
    is_f = (jnp.arange(rows) // heads) % 2 == 1
    return pl.pallas_call(
        _gates_body,
        grid=(t // tn,),
        in_specs=[pl.BlockSpec((rows, k), lambda i: (0, 0)),
                  pl.BlockSpec((tn, k), lambda i: (i, 0)),
                  pl.BlockSpec((rows, 1), lambda i: (0, 0)),
                  pl.BlockSpec((rows, 1), lambda i: (0, 0))],
        out_specs=pl.BlockSpec((rows, tn), lambda i: (0, i)),
        out_shape=jax.ShapeDtypeStruct((rows, t), F32),
        compiler_params=_params("parallel"),
        name=name,
    )(w_t, hn, bias.reshape(rows, 1).astype(F32), is_f.astype(F32).reshape(rows, 1))


def _conv_body(x_ref, prev_ref, next_ref, w_ref, o_ref, *, blocks_per_seq, q_blocks, q_scale):
    i = pl.program_id(0)
    j = pl.program_id(1)
    x = x_ref[...]
    ts = x.shape[0]
    pos = i % blocks_per_seq
    prev_row = jnp.where(pos == 0, 0.0, prev_ref[7:8, :])
    next_row = jnp.where(pos == blocks_per_seq - 1, 0.0, next_ref[0:1, :])
    row = lax.broadcasted_iota(jnp.int32, x.shape, 0)
    x_prev = jnp.where(row == 0, prev_row, pltpu.roll(x, 1, 0))
    x_next = jnp.where(row == ts - 1, next_row, pltpu.roll(x, ts - 1, 0))
    y = x_prev * w_ref[0:1, :] + x * w_ref[1:2, :] + x_next * w_ref[2:3, :]
    y = y * jax.nn.sigmoid(y)
    scale = jnp.where(j < q_blocks, q_scale, 1.0)
    o_ref[...] = (y * scale).astype(o_ref.dtype)


def _mlstm_conv(p, conv_w, *, batch, width, q_width, name="mlstm_conv"):
    t = p.shape[0]
    s_len = t // batch
    ts = _pick(s_len, (512, 256, 128))
    tc = _pick(q_width, (512, 256, 128))
    hb = ts // 8
    return pl.pallas_call(
        functools.partial(_conv_body, blocks_per_seq=s_len // ts, q_blocks=q_width // tc,
                          q_scale=MLSTM_QK_DIM ** -0.5),
        grid=(t // ts, width // tc),
        in_specs=[pl.BlockSpec((ts, tc), lambda i, j: (i, j)),
                  pl.BlockSpec((8, tc), lambda i, j: (jnp.maximum(i * hb - 1, 0), j)),
                  pl.BlockSpec((8, tc), lambda i, j: (jnp.minimum((i + 1) * hb, t // 8 - 1), j)),
                  pl.BlockSpec((3, tc), lambda i, j: (0, j))],
        out_specs=pl.BlockSpec((ts, tc), lambda i, j: (i, j)),
        out_shape=jax.ShapeDtypeStruct((t, width), BF16),
        compiler_params=_params("parallel", "parallel"),
        name=name,
    )(p, p, p, conv_w.astype(F32))


def _mlstm_chunk(q_ref, k_ref, v_ref, ig_ref, lf_ref, c_scr, n_scr, m_scr, *, backward):
    q = q_ref[...]
    k = k_ref[...]
    v = v_ref[...].astype(BF16)
    ig = ig_ref[0]
    lf = lf_ref[0]
    ln = q.shape[0]
    t_idx = lax.broadcasted_iota(jnp.int32, (ln, ln), 0)
    j_idx = lax.broadcasted_iota(jnp.int32, (ln, ln), 1)
    mask = (j_idx >= t_idx) if backward else (j_idx <= t_idx)
    eye = j_idx == t_idx
    lf_b = jnp.broadcast_to(lf, (ln, ln))
    ig_b = jnp.broadcast_to(ig, (ln, ln))
    cs_col = jnp.sum(jnp.where(mask, lf_b, 0.0), axis=1, keepdims=True)
    cs_row = jnp.sum(jnp.where(eye, cs_col, 0.0), axis=0, keepdims=True)
    ig_col = jnp.sum(jnp.where(eye, ig_b, 0.0), axis=1, keepdims=True)
    m_prev = m_scr[...]

    dmat = jnp.where(mask, cs_col - cs_row + ig, -jnp.inf)
    inter = cs_col + m_prev
    m_row = jnp.maximum(jnp.max(dmat, axis=1, keepdims=True), inter)
    s = _dot_nt(q, k) * jnp.exp(dmat - m_row)
    w_inter = jnp.exp(inter - m_row)
    num = _dot(s.astype(BF16), v) + w_inter * _dot(q, c_scr[...].astype(BF16))
    qn = jnp.sum(q.astype(F32) * n_scr[...], axis=1, keepdims=True)
    den = jnp.sum(s, axis=1, keepdims=True) + w_inter * qn
    h = num / jnp.maximum(jnp.abs(den), jnp.exp(-m_row))

    b_end = jnp.sum(lf, axis=1, keepdims=True)
    w_tok_row = b_end - cs_row + ig
    w_tok_col = b_end - cs_col + ig_col
    m_new = jnp.maximum(b_end + m_prev, jnp.max(w_tok_row, axis=1, keepdims=True))
    decay = jnp.exp(b_end + m_prev - m_new)
    wk = k.astype(F32) * jnp.exp(w_tok_col - m_new)
    c_scr[...] = decay * c_scr[...] + _dot_tn(wk.astype(BF16), v)
    n_scr[...] = decay * n_scr[...] + jnp.sum(wk, axis=0, keepdims=True)
    m_scr[...] = m_new
    return h


def _mlstm_scan_body(*refs):
    fwd_in, bwd_in = refs[0:5], refs[5:10]
    of_ref, ob_ref = refs[10:12]
    fwd_state, bwd_state = refs[12:15], refs[15:18]

    @pl.when(pl.program_id(1) == 0)
    def _():
        for c_scr, n_scr, m_scr in (fwd_state, bwd_state):
            c_scr[...] = jnp.zeros_like(c_scr)
            n_scr[...] = jnp.zeros_like(n_scr)
            m_scr[...] = jnp.full_like(m_scr, STAB_INIT)

    of_ref[...] = _mlstm_chunk(*fwd_in, *fwd_state, backward=False)
    ob_ref[...] = _mlstm_chunk(*bwd_in, *bwd_state, backward=True)


def _mlstm_scan(qk, p, gates, *, batch, heads, v_col, name="mlstm_scan"):
    t = qk.shape[0]
    s_len = t // batch
    ln = MLSTM_CHUNK
    nc = s_len // ln
    dk, dv = MLSTM_QK_DIM, MLSTM_V_DIM

    def specs(d):
        def chunk(c):
            return c + d * (nc - 1 - 2 * c)

        def row_block(bh, c):
            return (bh // heads) * nc + chunk(c)

        def gate_block(kind):
            return lambda bh, c: ((((2 * d + kind) * heads + bh % heads) * batch + bh // heads) * nc + chunk(c), 0, 0)

        ins = [pl.BlockSpec((ln, dk), lambda bh, c: (row_block(bh, c), bh % heads)),
               pl.BlockSpec((ln, dk), lambda bh, c: (row_block(bh, c), heads + bh % heads)),
               pl.BlockSpec((ln, dv), lambda bh, c: (row_block(bh, c), v_col // dv + bh % heads)),
               pl.BlockSpec((1, 1, ln), gate_block(0)),
               pl.BlockSpec((1, 1, ln), gate_block(1))]
        out = pl.BlockSpec((ln, dv), lambda bh, c: (row_block(bh, c), bh % heads))
        return ins, out

    f_in, f_out = specs(0)
    b_in, b_out = specs(1)
    state = [pltpu.VMEM((dk, dv), F32), pltpu.VMEM((1, dk), F32), pltpu.VMEM((1, 1), F32)]
    out_sds = jax.ShapeDtypeStruct((t, heads * dv), F32)
    return pl.pallas_call(
        _mlstm_scan_body,
        grid=(batch * heads, nc),
        in_specs=f_in + b_in,
        out_specs=[f_out, b_out],
        out_shape=[out_sds, out_sds],
        scratch_shapes=state + state,
        compiler_params=_params("parallel", "arbitrary"),
        name=name,
    )(*([qk, qk, p, gates, gates] * 2))


def _combine_body(hf_ref, hb_ref, o_pre_ref, g_ref, out_ref):
    h = hf_ref[...] + hb_ref[...]
    ms = jnp.mean(h * h, axis=-1, keepdims=True)
    y = h * lax.rsqrt(ms + EPS) * g_ref[...]
    out_ref[...] = (y * jax.nn.sigmoid(o_pre_ref[...])).astype(out_ref.dtype)


def _mlstm_combine(h_fwd, h_bwd, p, head_norm, *, heads, o_col, name="mlstm_combine"):
    t = p.shape[0]
    dv = MLSTM_V_DIM
    ts = _pick(t, (512, 256, 128))
    return pl.pallas_call(
        _combine_body,
        grid=(t // ts, heads),
        in_specs=[pl.BlockSpec((ts, dv), lambda i, h: (i, h)),
                  pl.BlockSpec((ts, dv), lambda i, h: (i, h)),
                  pl.BlockSpec((ts, dv), lambda i, h: (i, o_col // dv + h)),
                  pl.BlockSpec((1, dv), lambda i, h: (0, h))],
        out_specs=pl.BlockSpec((ts, dv), lambda i, h: (i, h)),
        out_shape=jax.ShapeDtypeStruct((t, heads * dv), BF16),
        compiler_params=_params("parallel", "parallel"),
        name=name,
    )(h_fwd, h_bwd, p, head_norm.reshape(1, heads * dv).astype(F32))


def _swap_halves(w):
    half = w.shape[-1] // 2
    return jnp.concatenate([w[..., half:], w[..., :half]], axis=-1)


def _prep_mla_weights(w_in, w_uq, q_rank, kv_rank, heads):
    a1 = q_rank + kv_rank
    a2 = a1 + MLA_ROPE_DIM
    w_kr = w_in[:, a1:a2]
    w_in_p = jnp.concatenate([w_in[:, :a1], w_in[:, a2:], w_kr, _swap_halves(w_kr)], axis=1)
    wq = w_uq.reshape(q_rank, heads, MLA_NOPE_DIM + MLA_ROPE_DIM)
    rope_cols = wq[..., MLA_NOPE_DIM:]
    w_uq_p = jnp.concatenate([wq[..., :MLA_NOPE_DIM], rope_cols, _swap_halves(rope_cols)], axis=-1)
    return w_in_p, w_uq_p.reshape(q_rank, heads * MLA_HEAD_COLS)


def _rope_tables(positions):
    inv_freq = 1.0 / (ROPE_THETA ** (jnp.arange(0, MLA_ROPE_DIM, 2, dtype=F32) / MLA_ROPE_DIM))
    ang = positions.astype(F32).reshape(-1, 1) * inv_freq
    cos, sin = jnp.cos(ang), jnp.sin(ang)
    k_tab = jnp.concatenate([cos, cos, -sin, sin], axis=1)
    scale = (MLA_NOPE_DIM + MLA_ROPE_DIM) ** -0.5 * math.log2(math.e)
    q_tab = jnp.concatenate([jnp.ones((ang.shape[0], MLA_NOPE_DIM), F32), k_tab], axis=1) * scale
    return q_tab, k_tab


def kernel(x, positions, mem, attn_norm, ffn_norm, mem_norm, final_norm, mla_w_in, mla_q_norm, mla_kv_norm,
           mla_w_uq, mla_w_ukv, mlstm_w_in, mlstm_conv_w, mlstm_gate_b, mlstm_head_norm, w_mem_kv, w_out,
           w_gu, w_down):
    batch, s_len, d_model = x.shape
    depth = attn_norm.shape[0]
    n_mem = mem.shape[1]
    t = batch * s_len
    mem_width = d_model // 4
    main_width = d_model - mem_width
    mla_heads = main_width // MLA_V_DIM
    mlstm_heads = main_width // MLSTM_V_DIM
    qk_width = mlstm_heads * MLSTM_QK_DIM
    v_width = mlstm_heads * MLSTM_V_DIM
    q_rank = mla_q_norm.shape[1]
    kv_rank = mla_kv_norm.shape[1]
    d_ff = w_down.shape[1]

    tm = _pick(t, (1024, 512, 256, 128))
    q_tab, k_tab = _rope_tables(positions)
    mem_n = _rmsnorm(mem.reshape(batch * n_mem, d_model), mem_norm, width=d_model, name="mem_norm")
    w_down_bf16 = w_down.astype(BF16)
    h = x.reshape(t, d_model)

    for i in range(depth):
        j = i // N_MIXERS
        hn = _rmsnorm(h, attn_norm[i], width=d_model, name="attn_norm")
        mkv = _matmul(mem_n, w_mem_kv, layer=i, out_dtype=BF16, tm=_pick(batch * n_mem, (512, 256, 128)),
                      tn=_pick(2 * mem_width, (512, 256)), name="mem_kv")
        if i % N_MIXERS == 0:
            w_in_p, w_uq_p = _prep_mla_weights(mla_w_in[j], mla_w_uq[j], q_rank, kv_rank, mla_heads)
            n_in = w_in_p.shape[1]
            p = _matmul(hn, w_in_p, out_dtype=F32, tm=tm, tn=_pick(n_in, (384, 256, 128)), name="mla_in_proj")
            cq = _rmsnorm(p, mla_q_norm[j], width=q_rank, col_block=0, name="mla_q_norm")
            ckv = _rmsnorm(p, mla_kv_norm[j], width=kv_rank, col_block=q_rank // kv_rank, name="mla_kv_norm")
            head_cols = mla_heads * MLA_HEAD_COLS
            q = _matmul_mul(cq, w_uq_p, q_tab, out_dtype=BF16, tm=tm, tn=_pick(head_cols, (1024, 768, 512, 256)),
                            name="mla_q_proj")
            kv = _matmul(ckv, mla_w_ukv, layer=j, out_dtype=BF16, tm=tm, tn=_pick(head_cols, (1024, 768, 512, 256)),
                         name="mla_kv_proj")
            q_mem_src, q_mem_col = p, q_rank + kv_rank
            kr = _krope(p, k_tab, col_block=(q_mem_col + mem_width) // (2 * MLA_ROPE_DIM))
            main = _flash_attention(q, kv, kr, batch=batch, heads=mla_heads)
        else:
            g_col = 2 * qk_width + 2 * v_width
            w_gates_t = mlstm_w_in[j, :, g_col:g_col + 4 * mlstm_heads].T.astype(BF16)
            w_q_mem = mlstm_w_in[j, :, g_col + 4 * mlstm_heads:]
            p = _matmul(hn, mlstm_w_in, layer=j, n=g_col, out_dtype=F32, tm=tm, tn=_pick(g_col, (512, 256, 128)),
                        name="mlstm_in_proj")
            q_mem_src = _matmul(hn, w_q_mem, out_dtype=BF16, tm=tm, tn=_pick(mem_width, (512, 256, 128)),
                                name="mlstm_qmem_proj")
            q_mem_col = 0
            gates = _mlstm_gates(w_gates_t, hn, mlstm_gate_b[j], heads=mlstm_heads)
            gates = gates.reshape(-1, 1, MLSTM_CHUNK)
            qk = _mlstm_conv(p, mlstm_conv_w[j], batch=batch, width=2 * qk_width, q_width=qk_width)
            h_fwd, h_bwd = _mlstm_scan(qk, p, gates, batch=batch, heads=mlstm_heads, v_col=2 * qk_width)
            main = _mlstm_combine(h_fwd, h_bwd, p, mlstm_head_norm[j], heads=mlstm_heads,
                                  o_col=2 * qk_width + v_width)
        mem_out = _memory_attention(q_mem_src, mkv, batch=batch, q_col=q_mem_col, width=mem_width)
        h = _matmul_res2(main, mem_out, w_out, h, layer=i, tm=tm, tn=_pick(d_model, (512, 256)), name="out_proj")
        hn = _rmsnorm(h, ffn_norm[i], width=d_model, name="ffn_norm")
        act = _matmul_swiglu(hn, w_gu, layer=i, tm=tm, tn=_pick(d_ff, (256, 128)), name="ffn_up")
        h = _matmul_res(act, w_down_bf16, h, layer=i, tm=_pick(t, (512, 256, 128)), tn=_pick(d_model, (512, 256)),
                        name="ffn_down")
    out = _rmsnorm(h, final_norm, width=d_model, out_dtype=x.dtype, name="final_norm")
    return out.reshape(batch, s_len, d_model)
```

```python
import functools
import math

import jax
import jax.numpy as jnp
from jax import lax
from jax.experimental import pallas as pl
from jax.experimental.pallas import tpu as pltpu

F32 = jnp.float32
BF16 = jnp.bfloat16

EPS = 1e-6
ROPE_THETA = 10000.0
MLA_NOPE_DIM = 128
MLA_ROPE_DIM = 64
MLA_V_DIM = 128
MLA_HEAD_COLS = 2 * MLA_NOPE_DIM
MLSTM_V_DIM = 512
MLSTM_QK_DIM = MLSTM_V_DIM // 2
MLSTM_CHUNK = 128
MEM_HEADS = 4
IGATE_CAP = 15.0
STAB_INIT = -1e30
N_MIXERS = 2

V7X_VMEM_BYTES = 64 * 1024 * 1024
VMEM_LIMIT_BYTES = V7X_VMEM_BYTES - 8 * 1024 * 1024


def _params(*semantics):
    return pltpu.CompilerParams(dimension_semantics=semantics, vmem_limit_bytes=VMEM_LIMIT_BYTES)


def _pick(n, candidates):
    for c in candidates:
        if n % c == 0:
            return c
    return n


def _dot(a, b):
    return jnp.dot(a, b, preferred_element_type=F32)


def _dot_nt(a, b):
    return lax.dot_general(a, b, (((1,), (1,)), ((), ())), preferred_element_type=F32)


def _dot_tn(a, b):
    return lax.dot_general(a, b, (((0,), (0,)), ((), ())), preferred_element_type=F32)


def _rmsnorm_body(x_ref, g_ref, o_ref):
    x = x_ref[...].astype(F32)
    ms = jnp.mean(x * x, axis=-1, keepdims=True)
    o_ref[...] = (x * lax.rsqrt(ms + EPS) * g_ref[...]).astype(o_ref.dtype)


def _rmsnorm(x, g, *, width, col_block=0, out_dtype=BF16, name="rmsnorm"):
    t = x.shape[0]
    tm = _pick(t, (256, 128, 64, 8))
    return pl.pallas_call(
        _rmsnorm_body,
        grid=(t // tm,),
        in_specs=[pl.BlockSpec((tm, width), lambda i: (i, col_block)),
                  pl.BlockSpec((1, width), lambda i: (0, 0))],
        out_specs=pl.BlockSpec((tm, width), lambda i: (i, 0)),
        out_shape=jax.ShapeDtypeStruct((t, width), out_dtype),
        compiler_params=_params("parallel"),
        name=name,
    )(x, g.reshape(1, width).astype(F32))


def _ep_plain(mm, a, w, extra, o_ref):
    o_ref[...] = mm(a[0][...].astype(BF16), w[0][...]).astype(o_ref.dtype)


def _ep_mul(mm, a, w, extra, o_ref):
    table = extra[0][...]
    reps = o_ref.shape[1] // table.shape[1]
    if reps > 1:
        table = jnp.concatenate([table] * reps, axis=1)
    o_ref[...] = (mm(a[0][...].astype(BF16), w[0][...]) * table).astype(o_ref.dtype)


def _ep_residual(mm, a, w, extra, o_ref):
    acc = mm(a[0][...], w[0][...])
    for a_ref, w_ref in zip(a[1:], w[1:]):
        acc = acc + mm(a_ref[...], w_ref[...])
    o_ref[...] = extra[0][...] + acc


def _ep_swiglu(mm, a, w, extra, o_ref):
    x = a[0][...]
    g = mm(x, w[0][...])
    u = mm(x, w[1][...])
    o_ref[...] = (g * jax.nn.sigmoid(g) * u).astype(o_ref.dtype)


def _mm_body(*refs, n_a, n_w, n_extra, epilogue, cast, transposed):
    a = refs[:n_a]
    w = refs[n_a:n_a + n_w]
    extra = refs[n_a + n_w:n_a + n_w + n_extra]
    o_ref = refs[n_a + n_w + n_extra]
    if cast:
        scratch = refs[n_a + n_w + n_extra + 1:]

        @pl.when(pl.program_id(1) == 0)
        def _():
            for w_ref, s_ref in zip(w, scratch):
                s_ref[...] = w_ref[...].astype(BF16)

        w = scratch
    epilogue(_dot_nt if transposed else _dot, a, w, extra, o_ref)


def _mm(name, epilogue, a_ops, w_ops, extra_ops, *, m, n, tm, tn, out_dtype, transposed=False):
    a_specs = [pl.BlockSpec(bs, im) for _, bs, im in a_ops]
    e_specs = [pl.BlockSpec(bs, im) for _, bs, im in extra_ops]

    def w_spec(layer, rows, rb, co):
        if transposed:
            return pl.BlockSpec((None, tn, rows), lambda j, i: (layer, j + co, rb))
        return pl.BlockSpec((None, rows, tn), lambda j, i: (layer, rb, j + co))

    w_specs = [w_spec(layer, rows, rb, co) for _, layer, rows, rb, co in w_ops]
    cast = w_ops[0][0].dtype != BF16
    scratch = []
    if cast:
        scratch = [pltpu.VMEM((tn, rows) if transposed else (rows, tn), BF16) for _, _, rows, _, _ in w_ops]
    return pl.pallas_call(
        functools.partial(_mm_body, n_a=len(a_ops), n_w=len(w_ops), n_extra=len(extra_ops),
                          epilogue=epilogue, cast=cast, transposed=transposed),
        grid=(n // tn, m // tm),
        in_specs=a_specs + w_specs + e_specs,
        out_specs=pl.BlockSpec((tm, tn), lambda j, i: (i, j)),
        out_shape=jax.ShapeDtypeStruct((m, n), out_dtype),
        scratch_shapes=scratch,
        compiler_params=_params("parallel", "arbitrary"),
        name=name,
    )(*[op[0] for op in a_ops], *[op[0] for op in w_ops], *[op[0] for op in extra_ops])


def _rows(tm, k, col_block=0):
    return (tm, k), lambda j, i: (i, col_block)


def _as3d(w):
    return w if w.ndim == 3 else w[None]


def _matmul(a, w, *, layer=0, n=None, out_dtype, tm, tn, name, transposed=False):
    w = _as3d(w)
    m, k = a.shape
    if n is None:
        n = w.shape[1] if transposed else w.shape[2]
    return _mm(name, _ep_plain, [(a, *_rows(tm, k))], [(w, layer, k, 0, 0)], [],
               m=m, n=n, tm=tm, tn=tn, out_dtype=out_dtype, transposed=transposed)


def _matmul_mul(a, w, table, *, out_dtype, tm, tn, name):
    w = _as3d(w)
    m, k = a.shape
    c = table.shape[1]
    return _mm(name, _ep_mul, [(a, *_rows(tm, k))], [(w, 0, k, 0, 0)], [(table, (tm, c), lambda j, i: (i, 0))],
               m=m, n=w.shape[2], tm=tm, tn=tn, out_dtype=out_dtype)


def _matmul_res(a, w, res, *, layer, tm, tn, name):
    m, k = a.shape
    return _mm(name, _ep_residual, [(a, *_rows(tm, k))], [(w, layer, k, 0, 0)],
               [(res, (tm, tn), lambda j, i: (i, j))], m=m, n=w.shape[2], tm=tm, tn=tn, out_dtype=F32)


def _matmul_res2(a1, a2, w, res, *, layer, tm, tn, name):
    m, k1 = a1.shape
    k2 = a2.shape[1]
    assert k1 % k2 == 0
    return _mm(name, _ep_residual, [(a1, *_rows(tm, k1)), (a2, *_rows(tm, k2))],
               [(w, layer, k1, 0, 0), (w, layer, k2, k1 // k2, 0)],
               [(res, (tm, tn), lambda j, i: (i, j))], m=m, n=w.shape[2], tm=tm, tn=tn, out_dtype=F32)


def _matmul_swiglu(a, w_gu, *, layer, tm, tn, name):
    m, k = a.shape
    f = w_gu.shape[2] // 2
    return _mm(name, _ep_swiglu, [(a, *_rows(tm, k))], [(w_gu, layer, k, 0, 0), (w_gu, layer, k, 0, f // tn)], [],
               m=m, n=f, tm=tm, tn=tn, out_dtype=BF16)


def _krope_body(x_ref, t_ref, o_ref):
    y = x_ref[...] * t_ref[...]
    o_ref[...] = (y + pltpu.roll(y, MLA_ROPE_DIM, 1)).astype(o_ref.dtype)


def _krope(p, table, *, col_block, name="mla_krope"):
    t = p.shape[0]
    tm = _pick(t, (1024, 512, 256, 128, 8))
    w = 2 * MLA_ROPE_DIM
    return pl.pallas_call(
        _krope_body,
        grid=(t // tm,),
        in_specs=[pl.BlockSpec((tm, w), lambda i: (i, col_block)),
                  pl.BlockSpec((tm, w), lambda i: (i, 0))],
        out_specs=pl.BlockSpec((tm, w), lambda i: (i, 0)),
        out_shape=jax.ShapeDtypeStruct((t, w), BF16),
        compiler_params=_params("parallel"),
        name=name,
    )(p, table)


def _flash_body(q_ref, kn_ref, v_ref, kr_ref, o_ref, k_scr, *, tk):
    @pl.when(pl.program_id(2) == 0)
    def _():
        k_scr[:, :MLA_NOPE_DIM] = kn_ref[...]
        k_scr[:, MLA_NOPE_DIM:] = kr_ref[...]

    q = q_ref[...]
    tq = q.shape[0]
    s_len = k_scr.shape[0]
    m = jnp.full((tq, 1), -jnp.inf, F32)
    l = jnp.zeros((tq, 1), F32)
    acc = jnp.zeros((tq, MLA_V_DIM), F32)
    for c in range(s_len // tk):
        s = _dot_nt(q, k_scr[c * tk:(c + 1) * tk, :])
        m_new = jnp.maximum(m, jnp.max(s, axis=-1, keepdims=True))
        alpha = jnp.exp2(m - m_new)
        p = jnp.exp2(s - m_new)
        l = alpha * l + jnp.sum(p, axis=-1, keepdims=True)
        acc = alpha * acc + _dot(p.astype(BF16), v_ref[c * tk:(c + 1) * tk, :])
        m = m_new
    o_ref[...] = (acc / l).astype(o_ref.dtype)


def _flash_attention(q, kv, kr, *, batch, heads, name="mla_flash"):
    t = q.shape[0]
    s_len = t // batch
    tq = _pick(s_len, (512, 256, 128))
    tk = _pick(s_len, (1024, 512, 256, 128))
    nq = s_len // tq
    return pl.pallas_call(
        functools.partial(_flash_body, tk=tk),
        grid=(batch, heads, nq),
        in_specs=[pl.BlockSpec((tq, MLA_HEAD_COLS), lambda b, h, i: (b * nq + i, h)),
                  pl.BlockSpec((s_len, MLA_NOPE_DIM), lambda b, h, i: (b, 2 * h)),
                  pl.BlockSpec((s_len, MLA_V_DIM), lambda b, h, i: (b, 2 * h + 1)),
                  pl.BlockSpec((s_len, 2 * MLA_ROPE_DIM), lambda b, h, i: (b, 0))],
        out_specs=pl.BlockSpec((tq, MLA_V_DIM), lambda b, h, i: (b * nq + i, h)),
        out_shape=jax.ShapeDtypeStruct((t, heads * MLA_V_DIM), BF16),
        scratch_shapes=[pltpu.VMEM((s_len, MLA_HEAD_COLS), BF16)],
        compiler_params=_params("parallel", "parallel", "arbitrary"),
        name=name,
    )(q, kv, kv, kr)


def _memattn_body(*refs, scale):
    q_refs = refs[:MEM_HEADS]
    mk_ref, mv_ref, o_ref = refs[MEM_HEADS:]
    dh = q_refs[0].shape[1]
    for h in range(MEM_HEADS):
        q = (q_refs[h][...].astype(F32) * scale).astype(BF16)
        s = _dot_nt(q, mk_ref[:, h * dh:(h + 1) * dh])
        p = jnp.exp(s - jnp.max(s, axis=-1, keepdims=True))
        l = jnp.sum(p, axis=-1, keepdims=True)
        o = _dot((p / l).astype(BF16), mv_ref[:, h * dh:(h + 1) * dh])
        o_ref[:, h * dh:(h + 1) * dh] = o.astype(o_ref.dtype)


def _memory_attention(p, mkv, *, batch, q_col, width, name="mem_attn"):
    t = p.shape[0]
    s_len = t // batch
    n_mem = mkv.shape[0] // batch
    dh = width // MEM_HEADS
    tq = _pick(s_len, (512, 256, 128))
    nq = s_len // tq
    q_specs = [pl.BlockSpec((tq, dh), functools.partial(lambda b, i, h: (b * nq + i, q_col // dh + h), h=h))
               for h in range(MEM_HEADS)]
    return pl.pallas_call(
        functools.partial(_memattn_body, scale=dh ** -0.5),
        grid=(batch, nq),
        in_specs=q_specs + [pl.BlockSpec((n_mem, width), lambda b, i: (b, 0)),
                            pl.BlockSpec((n_mem, width), lambda b, i: (b, 1))],
        out_specs=pl.BlockSpec((tq, width), lambda b, i: (b * nq + i, 0)),
        out_shape=jax.ShapeDtypeStruct((t, width), BF16),
        compiler_params=_params("parallel", "parallel"),
        name=name,
    )(*([p] * MEM_HEADS), mkv, mkv)


def _gates_body(w_ref, x_ref, b_ref, f_ref, o_ref):
    g = _dot_nt(w_ref[...].astype(BF16), x_ref[...]) + b_ref[...]
    igate = IGATE_CAP * jnp.tanh(g / IGATE_CAP)
    log_fgate = jnp.minimum(g, 0.0) - jnp.log1p(jnp.exp(-jnp.abs(g)))
    o_ref[...] = jnp.where(f_ref[...] > 0.5, log_fgate, igate)


def _mlstm_gates(w_t, hn, bias, *, heads, name="mlstm_gates"):
    rows, k = w_t.shape
    t = hn.shape[0]
    tn = _pick(t, (1024, 512, 256, 128))
    is_f = (jnp.arange(rows) // heads) % 2 == 1
    return pl.pallas_call(
        _gates_body,
        grid=(t // tn,),
        in_specs=[pl.BlockSpec((rows, k), lambda i: (0, 0)),
                  pl.BlockSpec((tn, k), lambda i: (i, 0)),
                  pl.BlockSpec((rows, 1), lambda i: (0, 0)),
                  pl.BlockSpec((rows, 1), lambda i: (0, 0))],
        out_specs=pl.BlockSpec((rows, tn), lambda i: (0, i)),
        out_shape=jax.ShapeDtypeStruct((rows, t), F32),
        compiler_params=_params("parallel"),
        name=name,
    )(w_t, hn, bias.reshape(rows, 1).astype(F32), is_f.astype(F32).reshape(rows, 1))


def _conv_body(x_ref, prev_ref, next_ref, w_ref, o_ref, *, blocks_per_seq, q_blocks, q_scale):
    i = pl.program_id(0)
    j = pl.program_id(1)
    x = x_ref[...]
    ts = x.shape[0]
    pos = i % blocks_per_seq
    prev_row = jnp.where(pos == 0, 0.0, prev_ref[7:8, :])
    next_row = jnp.where(pos == blocks_per_seq - 1, 0.0, next_ref[0:1, :])
    row = lax.broadcasted_iota(jnp.int32, x.shape, 0)
    x_prev = jnp.where(row == 0, prev_row, pltpu.roll(x, 1, 0))
    x_next = jnp.where(row == ts - 1, next_row, pltpu.roll(x, ts - 1, 0))
    y = x_prev * w_ref[0:1, :] + x * w_ref[1:2, :] + x_next * w_ref[2:3, :]
    y = y * jax.nn.sigmoid(y)
    scale = jnp.where(j < q_blocks, q_scale, 1.0)
    o_ref[...] = (y * scale).astype(o_ref.dtype)


def _mlstm_conv(p, conv_w, *, batch, width, q_width, name="mlstm_conv"):
    t = p.shape[0]
    s_len = t // batch
    ts = _pick(s_len, (512, 256, 128))
    tc = _pick(q_width, (512, 256, 128))
    hb = ts // 8
    return pl.pallas_call(
        functools.partial(_conv_body, blocks_per_seq=s_len // ts, q_blocks=q_width // tc,
                          q_scale=MLSTM_QK_DIM ** -0.5),
        grid=(t // ts, width // tc),
        in_specs=[pl.BlockSpec((ts, tc), lambda i, j: (i, j)),
                  pl.BlockSpec((8, tc), lambda i, j: (jnp.maximum(i * hb - 1, 0), j)),
                  pl.BlockSpec((8, tc), lambda i, j: (jnp.minimum((i + 1) * hb, t // 8 - 1), j)),
                  pl.BlockSpec((3, tc), lambda i, j: (0, j))],
        out_specs=pl.BlockSpec((ts, tc), lambda i, j: (i, j)),
        out_shape=jax.ShapeDtypeStruct((t, width), BF16),
        compiler_params=_params("parallel", "parallel"),
        name=name,
    )(p, p, p, conv_w.astype(F32))


def _mlstm_chunk(q, k, v, ig, lf, c_scr, n_scr, m_scr, *, backward):
    ln = q.shape[0]
    t_idx = lax.broadcasted_iota(jnp.int32, (ln, ln), 0)
    j_idx = lax.broadcasted_iota(jnp.int32, (ln, ln), 1)
    mask = (j_idx >= t_idx) if backward else (j_idx <= t_idx)
    eye = j_idx == t_idx
    lf_b = jnp.broadcast_to(lf, (ln, ln))
    ig_b = jnp.broadcast_to(ig, (ln, ln))
    cs_col = jnp.sum(jnp.where(mask, lf_b, 0.0), axis=1, keepdims=True)
    cs_row = jnp.sum(jnp.where(eye, cs_col, 0.0), axis=0, keepdims=True)
    ig_col = jnp.sum(jnp.where(eye, ig_b, 0.0), axis=1, keepdims=True)
    m_prev = m_scr[...]

    dmat = jnp.where(mask, cs_col - cs_row + ig, -jnp.inf)
    inter = cs_col + m_prev
    m_row = jnp.maximum(jnp.max(dmat, axis=1, keepdims=True), inter)
    s = _dot_nt(q, k) * jnp.exp(dmat - m_row)
    w_inter = jnp.exp(inter - m_row)
    num = _dot(s.astype(BF16), v) + w_inter * _dot(q, c_scr[...].astype(BF16))
    qn = jnp.sum(q.astype(F32) * n_scr[...], axis=1, keepdims=True)
    den = jnp.sum(s, axis=1, keepdims=True) + w_inter * qn
    h = num / jnp.maximum(jnp.abs(den), jnp.exp(-m_row))

    b_end = jnp.sum(lf, axis=1, keepdims=True)
    w_tok_row = b_end - cs_row + ig
    w_tok_col = b_end - cs_col + ig_col
    m_new = jnp.maximum(b_end + m_prev, jnp.max(w_tok_row, axis=1, keepdims=True))
    decay = jnp.exp(b_end + m_prev - m_new)
    wk = k.astype(F32) * jnp.exp(w_tok_col - m_new)
    c_scr[...] = decay * c_scr[...] + _dot_tn(wk.astype(BF16), v)
    n_scr[...] = decay * n_scr[...] + jnp.sum(wk, axis=0, keepdims=True)
    m_scr[...] = m_new
    return h


def _mlstm_scan_body(*refs, hps):
    n_in = 3 + 2 * hps
    dir_in = (refs[:n_in], refs[n_in:2 * n_in])
    out_refs = refs[2 * n_in:2 * n_in + 2]
    state = refs[2 * n_in + 2:]
    dk, dv = MLSTM_QK_DIM, MLSTM_V_DIM

    @pl.when(pl.program_id(1) == 0)
    def _():
        for idx in range(0, len(state), 3):
            c_scr, n_scr, m_scr = state[idx:idx + 3]
            c_scr[...] = jnp.zeros_like(c_scr)
            n_scr[...] = jnp.zeros_like(n_scr)
            m_scr[...] = jnp.full_like(m_scr, STAB_INIT)

    for d in range(2):
        q_ref, k_ref, v_ref = dir_in[d][:3]
        for hh in range(hps):
            ig_ref, lf_ref = dir_in[d][3 + 2 * hh:5 + 2 * hh]
            chain = 3 * (2 * hh + d)
            out_refs[d][:, hh * dv:(hh + 1) * dv] = _mlstm_chunk(
                q_ref[:, hh * dk:(hh + 1) * dk], k_ref[:, hh * dk:(hh + 1) * dk],
                v_ref[:, hh * dv:(hh + 1) * dv].astype(BF16), ig_ref[0], lf_ref[0],
                *state[chain:chain + 3], backward=bool(d))


def _mlstm_scan(qk, p, gates, *, batch, heads, v_col, name="mlstm_scan"):
    t = qk.shape[0]
    s_len = t // batch
    ln = MLSTM_CHUNK
    nc = s_len // ln
    dk, dv = MLSTM_QK_DIM, MLSTM_V_DIM
    hps = 2 if heads % 2 == 0 else 1
    groups = heads // hps

    def specs(d):
        def chunk(c):
            return c + d * (nc - 1 - 2 * c)

        def row_block(g, c):
            return (g // groups) * nc + chunk(c)

        def gate_block(kind, hh):
            def index(g, c):
                head = (g % groups) * hps + hh
                return (((2 * d + kind) * heads + head) * batch + g // groups) * nc + chunk(c), 0, 0
            return index

        ins = [pl.BlockSpec((ln, hps * dk), lambda g, c: (row_block(g, c), g % groups)),
               pl.BlockSpec((ln, hps * dk), lambda g, c: (row_block(g, c), groups + g % groups)),
               pl.BlockSpec((ln, hps * dv), lambda g, c: (row_block(g, c), v_col // (hps * dv) + g % groups))]
        for hh in range(hps):
            ins += [pl.BlockSpec((1, 1, ln), gate_block(0, hh)), pl.BlockSpec((1, 1, ln), gate_block(1, hh))]
        out = pl.BlockSpec((ln, hps * dv), lambda g, c: (row_block(g, c), g % groups))
        return ins, out

    f_in, f_out = specs(0)
    b_in, b_out = specs(1)
    state = [pltpu.VMEM((dk, dv), F32), pltpu.VMEM((1, dk), F32), pltpu.VMEM((1, 1), F32)]
    out_sds = jax.ShapeDtypeStruct((t, heads * dv), F32)
    operands = [qk, qk, p] + [gates, gates] * hps
    return pl.pallas_call(
        functools.partial(_mlstm_scan_body, hps=hps),
        grid=(batch * groups, nc),
        in_specs=f_in + b_in,
        out_specs=[f_out, b_out],
        out_shape=[out_sds, out_sds],
        scratch_shapes=state * (2 * hps),
        compiler_params=_params("parallel", "arbitrary"),
        name=name,
    )(*(operands * 2))


def _combine_body(hf_ref, hb_ref, o_pre_ref, g_ref, out_ref):
    h = hf_ref[...] + hb_ref[...]
    ms = jnp.mean(h * h, axis=-1, keepdims=True)
    y = h * lax.rsqrt(ms + EPS) * g_ref[...]
    out_ref[...] = (y * jax.nn.sigmoid(o_pre_ref[...])).astype(out_ref.dtype)


def _mlstm_combine(h_fwd, h_bwd, p, head_norm, *, heads, o_col, name="mlstm_combine"):
    t = p.shape[0]
    dv = MLSTM_V_DIM
    ts = _pick(t, (512, 256, 128))
    return pl.pallas_call(
        _combine_body,
        grid=(t // ts, heads),
        in_specs=[pl.BlockSpec((ts, dv), lambda i, h: (i, h)),
                  pl.BlockSpec((ts, dv), lambda i, h: (i, h)),
                  pl.BlockSpec((ts, dv), lambda i, h: (i, o_col // dv + h)),
                  pl.BlockSpec((1, dv), lambda i, h: (0, h))],
        out_specs=pl.BlockSpec((ts, dv), lambda i, h: (i, h)),
        out_shape=jax.ShapeDtypeStruct((t, heads * dv), BF16),
        compiler_params=_params("parallel", "parallel"),
        name=name,
    )(h_fwd, h_bwd, p, head_norm.reshape(1, heads * dv).astype(F32))


def _swap_halves(w, axis):
    half = w.shape[axis] // 2
    lo, hi = jnp.split(w, [half], axis=axis)
    return jnp.concatenate([hi, lo], axis=axis)


def _prep_mla_weights(w_in_t, w_uq, q_rank, kv_rank, heads):
    a1 = q_rank + kv_rank
    a2 = a1 + MLA_ROPE_DIM
    w_kr = w_in_t[a1:a2]
    w_in_p = jnp.concatenate([w_in_t[:a1], w_in_t[a2:], w_kr, _swap_halves(w_kr, 0)], axis=0).astype(BF16)
    wq = w_uq.reshape(q_rank, heads, MLA_NOPE_DIM + MLA_ROPE_DIM)
    rope_cols = wq[..., MLA_NOPE_DIM:]
    w_uq_p = jnp.concatenate([wq[..., :MLA_NOPE_DIM], rope_cols, _swap_halves(rope_cols, 2)], axis=-1)
    return w_in_p, w_uq_p.reshape(q_rank, heads * MLA_HEAD_COLS).astype(BF16)


def _rope_tables(positions):
    inv_freq = 1.0 / (ROPE_THETA ** (jnp.arange(0, MLA_ROPE_DIM, 2, dtype=F32) / MLA_ROPE_DIM))
    ang = positions.astype(F32).reshape(-1, 1) * inv_freq
    cos, sin = jnp.cos(ang), jnp.sin(ang)
    k_tab = jnp.concatenate([cos, cos, -sin, sin], axis=1)
    scale = (MLA_NOPE_DIM + MLA_ROPE_DIM) ** -0.5 * math.log2(math.e)
    q_tab = jnp.concatenate([jnp.ones((ang.shape[0], MLA_NOPE_DIM), F32), k_tab], axis=1) * scale
    return q_tab, k_tab


def kernel(x, positions, mem, attn_norm, ffn_norm, mem_norm, final_norm, mla_w_in, mla_q_norm, mla_kv_norm,
           mla_w_uq, mla_w_ukv, mlstm_w_in, mlstm_conv_w, mlstm_gate_b, mlstm_head_norm, w_mem_kv, w_out,
           w_gu, w_down):
    batch, s_len, d_model = x.shape
    depth = attn_norm.shape[0]
    n_mem = mem.shape[1]
    t = batch * s_len
    mem_width = d_model // 4
    main_width = d_model - mem_width
    mla_heads = main_width // MLA_V_DIM
    mlstm_heads = main_width // MLSTM_V_DIM
    qk_width = mlstm_heads * MLSTM_QK_DIM
    v_width = mlstm_heads * MLSTM_V_DIM
    q_rank = mla_q_norm.shape[1]
    kv_rank = mla_kv_norm.shape[1]
    d_ff = w_down.shape[1]

    tm = _pick(t, (1024, 512, 256, 128))
    q_tab, k_tab = _rope_tables(positions)
    mem_n = _rmsnorm(mem.reshape(batch * n_mem, d_model), mem_norm, width=d_model, name="mem_norm")
    w_down_bf16 = w_down.astype(BF16)
    mla_w_in_t = jnp.swapaxes(mla_w_in, 1, 2)
    mlstm_w_in_t = jnp.swapaxes(mlstm_w_in, 1, 2)
    h = x.reshape(t, d_model)

    for i in range(depth):
        j = i // N_MIXERS
        hn = _rmsnorm(h, attn_norm[i], width=d_model, name="attn_norm")
        mkv = _matmul(mem_n, w_mem_kv, layer=i, out_dtype=BF16, tm=_pick(batch * n_mem, (512, 256, 128)),
                      tn=_pick(2 * mem_width, (512, 256)), name="mem_kv")
        if i % N_MIXERS == 0:
            w_in_p, w_uq_p = _prep_mla_weights(mla_w_in_t[j], mla_w_uq[j], q_rank, kv_rank, mla_heads)
            n_in = w_in_p.shape[0]
            p = _matmul(hn, w_in_p, out_dtype=F32, tm=tm, tn=_pick(n_in, (896, 768, 640, 512, 384, 256, 128)),
                        name="mla_in_proj", transposed=True)
            cq = _rmsnorm(p, mla_q_norm[j], width=q_rank, col_block=0, name="mla_q_norm")
            ckv = _rmsnorm(p, mla_kv_norm[j], width=kv_rank, col_block=q_rank // kv_rank, name="mla_kv_norm")
            head_cols = mla_heads * MLA_HEAD_COLS
            q = _matmul_mul(cq, w_uq_p, q_tab, out_dtype=BF16, tm=tm, tn=_pick(head_cols, (1024, 768, 512, 256)),
                            name="mla_q_proj")
            kv = _matmul(ckv, mla_w_ukv, layer=j, out_dtype=BF16, tm=tm, tn=_pick(head_cols, (1024, 768, 512, 256)),
                         name="mla_kv_proj")
            q_mem_src, q_mem_col = p, q_rank + kv_rank
            kr = _krope(p, k_tab, col_block=(q_mem_col + mem_width) // (2 * MLA_ROPE_DIM))
            main = _flash_attention(q, kv, kr, batch=batch, heads=mla_heads)
        else:
            g_col = 2 * qk_width + 2 * v_width
            w_gates_t = mlstm_w_in_t[j, g_col:g_col + 4 * mlstm_heads]
            w_q_mem_t = mlstm_w_in_t[j, g_col + 4 * mlstm_heads:]
            p = _matmul(hn, mlstm_w_in_t, layer=j, n=g_col, out_dtype=F32, tm=tm, tn=_pick(g_col, (512, 256, 128)),
                        name="mlstm_in_proj", transposed=True)
            q_mem_src = _matmul(hn, w_q_mem_t, out_dtype=BF16, tm=tm, tn=_pick(mem_width, (512, 256, 128)),
                                name="mlstm_qmem_proj", transposed=True)
            q_mem_col = 0
            gates = _mlstm_gates(w_gates_t, hn, mlstm_gate_b[j], heads=mlstm_heads)
            gates = gates.reshape(-1, 1, MLSTM_CHUNK)
            qk = _mlstm_conv(p, mlstm_conv_w[j], batch=batch, width=2 * qk_width, q_width=qk_width)
            h_fwd, h_bwd = _mlstm_scan(qk, p, gates, batch=batch, heads=mlstm_heads, v_col=2 * qk_width)
            main = _mlstm_combine(h_fwd, h_bwd, p, mlstm_head_norm[j], heads=mlstm_heads,
                                  o_col=2 * qk_width + v_width)
        mem_out = _memory_attention(q_mem_src, mkv, batch=batch, q_col=q_mem_col, width=mem_width)
        h = _matmul_res2(main, mem_out, w_out, h, layer=i, tm=tm, tn=_pick(d_model, (512, 256)), name="out_proj")
        hn = _rmsnorm(h, ffn_norm[i], width=d_model, name="ffn_norm")
        act = _matmul_swiglu(hn, w_gu, layer=i, tm=tm, tn=_pick(d_ff, (256, 128)), name="ffn_up")
        h = _matmul_res(act, w_down_bf16, h, layer=i, tm=_pick(t, (512, 256, 128)), tn=_pick(d_model, (512, 256)),
                        name="ffn_down")
    out = _rmsnorm(h, final_norm, width=d_model, out_dtype=x.dtype, name="final_norm")
    return out.reshape(batch, s_len, d_model)
```

```python
import functools
import math

import jax
import jax.numpy as jnp
from jax import lax
from jax.experimental import pallas as pl
from jax.experimental.pallas import tpu as pltpu

F32 = jnp.float32
BF16 = jnp.bfloat16

EPS = 1e-6
ROPE_THETA = 10000.0
MLA_NOPE_DIM = 128
MLA_ROPE_DIM = 64
MLA_V_DIM = 128
MLA_HEAD_COLS = 2 * MLA_NOPE_DIM
MLSTM_V_DIM = 512
MLSTM_QK_DIM = MLSTM_V_DIM // 2
MLSTM_CHUNK = 128
MEM_HEADS = 4
IGATE_CAP = 15.0
STAB_INIT = -1e30
N_MIXERS = 2

V7X_VMEM_BYTES = 64 * 1024 * 1024
VMEM_LIMIT_BYTES = V7X_VMEM_BYTES - 8 * 1024 * 1024


def _params(*semantics):
    return pltpu.CompilerParams(dimension_semantics=semantics, vmem_limit_bytes=VMEM_LIMIT_BYTES)


def _pick(n, candidates):
    for c in candidates:
        if n % c == 0:
            return c
    return n


def _dot(a, b):
    return jnp.dot(a, b, preferred_element_type=F32)


def _dot_nt(a, b):
    return lax.dot_general(a, b, (((1,), (1,)), ((), ())), preferred_element_type=F32)


def _dot_tn(a, b):
    return lax.dot_general(a, b, (((0,), (0,)), ((), ())), preferred_element_type=F32)


def _rmsnorm_body(x_ref, g_ref, o_ref):
    x = x_ref[...].astype(F32)
    ms = jnp.mean(x * x, axis=-1, keepdims=True)
    o_ref[...] = (x * lax.rsqrt(ms + EPS) * g_ref[...]).astype(o_ref.dtype)


def _rmsnorm(x, g, *, width, col_block=0, out_dtype=BF16, name="rmsnorm"):
    t = x.shape[0]
    tm = _pick(t, (256, 128, 64, 8))
    return pl.pallas_call(
        _rmsnorm_body,
        grid=(t // tm,),
        in_specs=[pl.BlockSpec((tm, width), lambda i: (i, col_block)),
                  pl.BlockSpec((1, width), lambda i: (0, 0))],
        out_specs=pl.BlockSpec((tm, width), lambda i: (i, 0)),
        out_shape=jax.ShapeDtypeStruct((t, width), out_dtype),
        compiler_params=_params("parallel"),
        name=name,
    )(x, g.reshape(1, width).astype(F32))


def _ep_plain(mm, a, w, extra, o_ref):
    o_ref[...] = mm(a[0].astype(BF16), w[0][...]).astype(o_ref.dtype)


def _ep_mul(mm, a, w, extra, o_ref):
    table = extra[0][...]
    reps = o_ref.shape[1] // table.shape[1]
    if reps > 1:
        table = jnp.concatenate([table] * reps, axis=1)
    o_ref[...] = (mm(a[0].astype(BF16), w[0][...]) * table).astype(o_ref.dtype)


def _ep_residual(mm, a, w, extra, o_ref):
    acc = mm(a[0], w[0][...])
    for a_val, w_ref in zip(a[1:], w[1:]):
        acc = acc + mm(a_val, w_ref[...])
    o_ref[...] = extra[0][...] + acc


def _ep_swiglu(mm, a, w, extra, o_ref):
    g = mm(a[0], w[0][...])
    u = mm(a[0], w[1][...])
    o_ref[...] = (g * jax.nn.sigmoid(g) * u).astype(o_ref.dtype)


RING_SLOTS = 3


def _ring_fetch(a_hbm, bufs, sems, *, tm, n_m, total):
    step = pl.program_id(0) * n_m + pl.program_id(1)

    def copy(op, st):
        row = (st % n_m) * tm
        if not isinstance(row, int):
            row = pl.multiple_of(row, tm)
        slot = st % RING_SLOTS
        return pltpu.make_async_copy(a_hbm[op].at[pl.ds(row, tm), :], bufs[op].at[slot], sems.at[op, slot])

    @pl.when(step == 0)
    def _():
        for op in range(len(a_hbm)):
            for st in range(min(RING_SLOTS - 1, total)):
                copy(op, st).start()

    @pl.when(step + (RING_SLOTS - 1) < total)
    def _():
        for op in range(len(a_hbm)):
            copy(op, step + (RING_SLOTS - 1)).start()

    for op in range(len(a_hbm)):
        copy(op, step).wait()
    slot = step % RING_SLOTS
    return [buf[slot] for buf in bufs]


def _mm_body(*refs, n_a, n_w, n_extra, epilogue, cast, transposed, ring):
    a = refs[:n_a]
    w = refs[n_a:n_a + n_w]
    extra = refs[n_a + n_w:n_a + n_w + n_extra]
    o_ref = refs[n_a + n_w + n_extra]
    scratch = refs[n_a + n_w + n_extra + 1:]
    if cast:
        w_scratch, scratch = scratch[:n_w], scratch[n_w:]

        @pl.when(pl.program_id(1) == 0)
        def _():
            for w_ref, s_ref in zip(w, w_scratch):
                s_ref[...] = w_ref[...].astype(BF16)

        w = w_scratch
    if ring is None:
        a_vals = [a_ref[...] for a_ref in a]
    else:
        a_vals = _ring_fetch(a, scratch[:n_a], scratch[n_a], **ring)
    epilogue(_dot_nt if transposed else _dot, a_vals, w, extra, o_ref)


def _mm(name, epilogue, a_ops, w_ops, extra_ops, *, m, n, tm, tn, out_dtype, transposed=False, ring=False):
    e_specs = [pl.BlockSpec(bs, im) for _, bs, im in extra_ops]
    n_m = m // tm
    ring_args = None
    ring_scratch = []
    if ring:
        a_specs = [pl.BlockSpec(memory_space=pl.ANY) for _ in a_ops]
        ring_args = dict(tm=tm, n_m=n_m, total=(n // tn) * n_m)
        ring_scratch = [pltpu.VMEM((RING_SLOTS,) + tuple(bs), arr.dtype) for arr, bs, _ in a_ops]
        ring_scratch.append(pltpu.SemaphoreType.DMA((len(a_ops), RING_SLOTS)))
    else:
        a_specs = [pl.BlockSpec(bs, im) for _, bs, im in a_ops]

    def w_spec(layer, rows, rb, co):
        if transposed:
            return pl.BlockSpec((None, tn, rows), lambda j, i: (layer, j + co, rb))
        return pl.BlockSpec((None, rows, tn), lambda j, i: (layer, rb, j + co))

    w_specs = [w_spec(layer, rows, rb, co) for _, layer, rows, rb, co in w_ops]
    cast = w_ops[0][0].dtype != BF16
    scratch = []
    if cast:
        scratch = [pltpu.VMEM((tn, rows) if transposed else (rows, tn), BF16) for _, _, rows, _, _ in w_ops]
    return pl.pallas_call(
        functools.partial(_mm_body, n_a=len(a_ops), n_w=len(w_ops), n_extra=len(extra_ops),
                          epilogue=epilogue, cast=cast, transposed=transposed, ring=ring_args),
        grid=(n // tn, n_m),
        in_specs=a_specs + w_specs + e_specs,
        out_specs=pl.BlockSpec((tm, tn), lambda j, i: (i, j)),
        out_shape=jax.ShapeDtypeStruct((m, n), out_dtype),
        scratch_shapes=scratch + ring_scratch,
        compiler_params=_params("arbitrary", "arbitrary"),
        name=name,
    )(*[op[0] for op in a_ops], *[op[0] for op in w_ops], *[op[0] for op in extra_ops])


def _rows(tm, k, col_block=0):
    return (tm, k), lambda j, i: (i, col_block)


def _as3d(w):
    return w if w.ndim == 3 else w[None]


def _matmul(a, w, *, layer=0, n=None, out_dtype, tm, tn, name, transposed=False, ring=False):
    w = _as3d(w)
    m, k = a.shape
    if n is None:
        n = w.shape[1] if transposed else w.shape[2]
    return _mm(name, _ep_plain, [(a, *_rows(tm, k))], [(w, layer, k, 0, 0)], [],
               m=m, n=n, tm=tm, tn=tn, out_dtype=out_dtype, transposed=transposed, ring=ring)


def _matmul_mul(a, w, table, *, out_dtype, tm, tn, name):
    w = _as3d(w)
    m, k = a.shape
    c = table.shape[1]
    return _mm(name, _ep_mul, [(a, *_rows(tm, k))], [(w, 0, k, 0, 0)], [(table, (tm, c), lambda j, i: (i, 0))],
               m=m, n=w.shape[2], tm=tm, tn=tn, out_dtype=out_dtype)


def _matmul_res(a, w, res, *, layer, tm, tn, name):
    m, k = a.shape
    return _mm(name, _ep_residual, [(a, *_rows(tm, k))], [(w, layer, k, 0, 0)],
               [(res, (tm, tn), lambda j, i: (i, j))], m=m, n=w.shape[2], tm=tm, tn=tn, out_dtype=F32)


def _matmul_res2(a1, a2, w, res, *, layer, tm, tn, name, ring=False):
    m, k1 = a1.shape
    k2 = a2.shape[1]
    assert k1 % k2 == 0
    return _mm(name, _ep_residual, [(a1, *_rows(tm, k1)), (a2, *_rows(tm, k2))],
               [(w, layer, k1, 0, 0), (w, layer, k2, k1 // k2, 0)],
               [(res, (tm, tn), lambda j, i: (i, j))], m=m, n=w.shape[2], tm=tm, tn=tn, out_dtype=F32, ring=ring)


def _matmul_swiglu(a, w_gu, *, layer, tm, tn, name, ring=False):
    m, k = a.shape
    f = w_gu.shape[2] // 2
    return _mm(name, _ep_swiglu, [(a, *_rows(tm, k))], [(w_gu, layer, k, 0, 0), (w_gu, layer, k, 0, f // tn)], [],
               m=m, n=f, tm=tm, tn=tn, out_dtype=BF16, ring=ring)


def _krope_body(x_ref, t_ref, o_ref):
    y = x_ref[...] * t_ref[...]
    o_ref[...] = (y + pltpu.roll(y, MLA_ROPE_DIM, 1)).astype(o_ref.dtype)


def _krope(p, table, *, col_block, name="mla_krope"):
    t = p.shape[0]
    tm = _pick(t, (1024, 512, 256, 128, 8))
    w = 2 * MLA_ROPE_DIM
    return pl.pallas_call(
        _krope_body,
        grid=(t // tm,),
        in_specs=[pl.BlockSpec((tm, w), lambda i: (i, col_block)),
                  pl.BlockSpec((tm, w), lambda i: (i, 0))],
        out_specs=pl.BlockSpec((tm, w), lambda i: (i, 0)),
        out_shape=jax.ShapeDtypeStruct((t, w), BF16),
        compiler_params=_params("parallel"),
        name=name,
    )(p, table)


def _flash_body(q_ref, kn_ref, v_ref, kr_ref, o_ref, k_scr, *, tk):
    @pl.when(pl.program_id(2) == 0)
    def _():
        k_scr[:, :MLA_NOPE_DIM] = kn_ref[...]
        k_scr[:, MLA_NOPE_DIM:] = kr_ref[...]

    q = q_ref[...]
    tq = q.shape[0]
    s_len = k_scr.shape[0]
    m = jnp.full((tq, 1), -jnp.inf, F32)
    l = jnp.zeros((tq, 1), F32)
    acc = jnp.zeros((tq, MLA_V_DIM), F32)
    for c in range(s_len // tk):
        s = _dot_nt(q, k_scr[c * tk:(c + 1) * tk, :])
        m_new = jnp.maximum(m, jnp.max(s, axis=-1, keepdims=True))
        alpha = jnp.exp2(m - m_new)
        p = jnp.exp2(s - m_new)
        l = alpha * l + jnp.sum(p, axis=-1, keepdims=True)
        acc = alpha * acc + _dot(p.astype(BF16), v_ref[c * tk:(c + 1) * tk, :])
        m = m_new
    o_ref[...] = (acc / l).astype(o_ref.dtype)


def _flash_attention(q, kv, kr, *, batch, heads, name="mla_flash"):
    t = q.shape[0]
    s_len = t // batch
    tq = _pick(s_len, (1024, 512, 256, 128))
    tk = _pick(s_len, (1024, 512, 256, 128))
    nq = s_len // tq
    return pl.pallas_call(
        functools.partial(_flash_body, tk=tk),
        grid=(batch, heads, nq),
        in_specs=[pl.BlockSpec((tq, MLA_HEAD_COLS), lambda b, h, i: (b * nq + i, h)),
                  pl.BlockSpec((s_len, MLA_NOPE_DIM), lambda b, h, i: (b, 2 * h)),
                  pl.BlockSpec((s_len, MLA_V_DIM), lambda b, h, i: (b, 2 * h + 1)),
                  pl.BlockSpec((s_len, 2 * MLA_ROPE_DIM), lambda b, h, i: (b, 0))],
        out_specs=pl.BlockSpec((tq, MLA_V_DIM), lambda b, h, i: (b * nq + i, h)),
        out_shape=jax.ShapeDtypeStruct((t, heads * MLA_V_DIM), BF16),
        scratch_shapes=[pltpu.VMEM((s_len, MLA_HEAD_COLS), BF16)],
        compiler_params=_params("parallel", "parallel", "arbitrary"),
        name=name,
    )(q, kv, kv, kr)


def _memattn_body(*refs, scale):
    q_refs = refs[:MEM_HEADS]
    mk_ref, mv_ref, o_ref = refs[MEM_HEADS:]
    dh = q_refs[0].shape[1]
    for h in range(MEM_HEADS):
        q = (q_refs[h][...].astype(F32) * scale).astype(BF16)
        s = _dot_nt(q, mk_ref[:, h * dh:(h + 1) * dh])
        p = jnp.exp(s - jnp.max(s, axis=-1, keepdims=True))
        l = jnp.sum(p, axis=-1, keepdims=True)
        o = _dot((p / l).astype(BF16), mv_ref[:, h * dh:(h + 1) * dh])
        o_ref[:, h * dh:(h + 1) * dh] = o.astype(o_ref.dtype)


def _memory_attention(p, mkv, *, batch, q_col, width, name="mem_attn"):
    t = p.shape[0]
    s_len = t // batch
    n_mem = mkv.shape[0] // batch
    dh = width // MEM_HEADS
    tq = _pick(s_len, (512, 256, 128))
    nq = s_len // tq
    q_specs = [pl.BlockSpec((tq, dh), functools.partial(lambda b, i, h: (b * nq + i, q_col // dh + h), h=h))
               for h in range(MEM_HEADS)]
    return pl.pallas_call(
        functools.partial(_memattn_body, scale=dh ** -0.5),
        grid=(batch, nq),
        in_specs=q_specs + [pl.BlockSpec((n_mem, width), lambda b, i: (b, 0)),
                            pl.BlockSpec((n_mem, width), lambda b, i: (b, 1))],
        out_specs=pl.BlockSpec((tq, width), lambda b, i: (b * nq + i, 0)),
        out_shape=jax.ShapeDtypeStruct((t, width), BF16),
        compiler_params=_params("parallel", "parallel"),
        name=name,
    )(*([p] * MEM_HEADS), mkv, mkv)


def _gates_body(w_ref, x_ref, b_ref, f_ref, o_ref):
    g = _dot_nt(w_ref[...].astype(BF16), x_ref[...]) + b_ref[...]
    igate = IGATE_CAP * jnp.tanh(g / IGATE_CAP)
    log_fgate = jnp.minimum(g, 0.0) - jnp.log1p(jnp.exp(-jnp.abs(g)))
    o_ref[...] = jnp.where(f_ref[...] > 0.5, log_fgate, igate)


def _mlstm_gates(w_t, hn, bias, *, heads, name="mlstm_gates"):
    rows, k = w_t.shape
    t = hn.shape[0]
    tn = _pick(t, (1024, 512, 256, 128))
    is_f = (jnp.arange(rows) // heads) % 2 == 1
    return pl.pallas_call(
        _gates_body,
        grid=(t // tn,),
        in_specs=[pl.BlockSpec((rows, k), lambda i: (0, 0)),
                  pl.BlockSpec((tn, k), lambda i: (i, 0)),
                  pl.BlockSpec((rows, 1), lambda i: (0, 0)),
                  pl.BlockSpec((rows, 1), lambda i: (0, 0))],
        out_specs=pl.BlockSpec((rows, tn), lambda i: (0, i)),
        out_shape=jax.ShapeDtypeStruct((rows, t), F32),
        compiler_params=_params("parallel"),
        name=name,
    )(w_t, hn, bias.reshape(rows, 1).astype(F32), is_f.astype(F32).reshape(rows, 1))


def _conv_body(x_ref, prev_ref, next_ref, w_ref, o_ref, *, blocks_per_seq, q_blocks, q_scale):
    i = pl.program_id(0)
    j = pl.program_id(1)
    x = x_ref[...]
    ts = x.shape[0]
    pos = i % blocks_per_seq
    prev_row = jnp.where(pos == 0, 0.0, prev_ref[7:8, :])
    next_row = jnp.where(pos == blocks_per_seq - 1, 0.0, next_ref[0:1, :])
    row = lax.broadcasted_iota(jnp.int32, x.shape, 0)
    x_prev = jnp.where(row == 0, prev_row, pltpu.roll(x, 1, 0))
    x_next = jnp.where(row == ts - 1, next_row, pltpu.roll(x, ts - 1, 0))
    y = x_prev * w_ref[0:1, :] + x * w_ref[1:2, :] + x_next * w_ref[2:3, :]
    y = y * jax.nn.sigmoid(y)
    scale = jnp.where(j < q_blocks, q_scale, 1.0)
    o_ref[...] = (y * scale).astype(o_ref.dtype)


def _mlstm_conv(p, conv_w, *, batch, width, q_width, name="mlstm_conv"):
    t = p.shape[0]
    s_len = t // batch
    ts = _pick(s_len, (512, 256, 128))
    tc = _pick(q_width, (512, 256, 128))
    hb = ts // 8
    return pl.pallas_call(
        functools.partial(_conv_body, blocks_per_seq=s_len // ts, q_blocks=q_width // tc,
                          q_scale=MLSTM_QK_DIM ** -0.5),
        grid=(t // ts, width // tc),
        in_specs=[pl.BlockSpec((ts, tc), lambda i, j: (i, j)),
                  pl.BlockSpec((8, tc), lambda i, j: (jnp.maximum(i * hb - 1, 0), j)),
                  pl.BlockSpec((8, tc), lambda i, j: (jnp.minimum((i + 1) * hb, t // 8 - 1), j)),
                  pl.BlockSpec((3, tc), lambda i, j: (0, j))],
        out_specs=pl.BlockSpec((ts, tc), lambda i, j: (i, j)),
        out_shape=jax.ShapeDtypeStruct((t, width), BF16),
        compiler_params=_params("parallel", "parallel"),
        name=name,
    )(p, p, p, conv_w.astype(F32))


def _mlstm_chunk(q, k, v, ig, lf, c_scr, n_scr, m_scr, *, backward):
    ln = q.shape[0]
    t_idx = lax.broadcasted_iota(jnp.int32, (ln, ln), 0)
    j_idx = lax.broadcasted_iota(jnp.int32, (ln, ln), 1)
    mask = (j_idx >= t_idx) if backward else (j_idx <= t_idx)
    eye = j_idx == t_idx
    lf_b = jnp.broadcast_to(lf, (ln, ln))
    ig_b = jnp.broadcast_to(ig, (ln, ln))
    cs_col = jnp.sum(jnp.where(mask, lf_b, 0.0), axis=1, keepdims=True)
    cs_row = jnp.sum(jnp.where(eye, cs_col, 0.0), axis=0, keepdims=True)
    ig_col = jnp.sum(jnp.where(eye, ig_b, 0.0), axis=1, keepdims=True)
    m_prev = m_scr[...]

    dmat = jnp.where(mask, cs_col - cs_row + ig, -jnp.inf)
    inter = cs_col + m_prev
    m_row = jnp.maximum(jnp.max(dmat, axis=1, keepdims=True), inter)
    s = _dot_nt(q, k) * jnp.exp(dmat - m_row)
    w_inter = jnp.exp(inter - m_row)
    num = _dot(s.astype(BF16), v) + w_inter * _dot(q, c_scr[...].astype(BF16))
    qn = jnp.sum(q.astype(F32) * n_scr[...], axis=1, keepdims=True)
    den = jnp.sum(s, axis=1, keepdims=True) + w_inter * qn
    h = num / jnp.maximum(jnp.abs(den), jnp.exp(-m_row))

    b_end = jnp.sum(lf, axis=1, keepdims=True)
    w_tok_row = b_end - cs_row + ig
    w_tok_col = b_end - cs_col + ig_col
    m_new = jnp.maximum(b_end + m_prev, jnp.max(w_tok_row, axis=1, keepdims=True))
    decay = jnp.exp(b_end + m_prev - m_new)
    wk = k.astype(F32) * jnp.exp(w_tok_col - m_new)
    c_scr[...] = decay * c_scr[...] + _dot_tn(wk.astype(BF16), v)
    n_scr[...] = decay * n_scr[...] + jnp.sum(wk, axis=0, keepdims=True)
    m_scr[...] = m_new
    return h


def _mlstm_scan_body(*refs, hps):
    n_in = 3 + 2 * hps
    dir_in = (refs[:n_in], refs[n_in:2 * n_in])
    out_refs = refs[2 * n_in:2 * n_in + 2]
    state = refs[2 * n_in + 2:]
    dk, dv = MLSTM_QK_DIM, MLSTM_V_DIM

    @pl.when(pl.program_id(1) == 0)
    def _():
        for idx in range(0, len(state), 3):
            c_scr, n_scr, m_scr = state[idx:idx + 3]
            c_scr[...] = jnp.zeros_like(c_scr)
            n_scr[...] = jnp.zeros_like(n_scr)
            m_scr[...] = jnp.full_like(m_scr, STAB_INIT)

    for d in range(2):
        q_ref, k_ref, v_ref = dir_in[d][:3]
        for hh in range(hps):
            ig_ref, lf_ref = dir_in[d][3 + 2 * hh:5 + 2 * hh]
            chain = 3 * (2 * hh + d)
            out_refs[d][:, hh * dv:(hh + 1) * dv] = _mlstm_chunk(
                q_ref[:, hh * dk:(hh + 1) * dk], k_ref[:, hh * dk:(hh + 1) * dk],
                v_ref[:, hh * dv:(hh + 1) * dv].astype(BF16), ig_ref[0], lf_ref[0],
                *state[chain:chain + 3], backward=bool(d))


def _mlstm_scan(qk, p, gates, *, batch, heads, v_col, name="mlstm_scan"):
    t = qk.shape[0]
    s_len = t // batch
    ln = MLSTM_CHUNK
    nc = s_len // ln
    dk, dv = MLSTM_QK_DIM, MLSTM_V_DIM
    hps = 2 if heads % 2 == 0 else 1
    groups = heads // hps

    def specs(d):
        def chunk(c):
            return c + d * (nc - 1 - 2 * c)

        def row_block(g, c):
            return (g // groups) * nc + chunk(c)

        def gate_block(kind, hh):
            def index(g, c):
                head = (g % groups) * hps + hh
                return (((2 * d + kind) * heads + head) * batch + g // groups) * nc + chunk(c), 0, 0
            return index

        ins = [pl.BlockSpec((ln, hps * dk), lambda g, c: (row_block(g, c), g % groups)),
               pl.BlockSpec((ln, hps * dk), lambda g, c: (row_block(g, c), groups + g % groups)),
               pl.BlockSpec((ln, hps * dv), lambda g, c: (row_block(g, c), v_col // (hps * dv) + g % groups))]
        for hh in range(hps):
            ins += [pl.BlockSpec((1, 1, ln), gate_block(0, hh)), pl.BlockSpec((1, 1, ln), gate_block(1, hh))]
        out = pl.BlockSpec((ln, hps * dv), lambda g, c: (row_block(g, c), g % groups))
        return ins, out

    f_in, f_out = specs(0)
    b_in, b_out = specs(1)
    state = [pltpu.VMEM((dk, dv), F32), pltpu.VMEM((1, dk), F32), pltpu.VMEM((1, 1), F32)]
    out_sds = jax.ShapeDtypeStruct((t, heads * dv), F32)
    operands = [qk, qk, p] + [gates, gates] * hps
    return pl.pallas_call(
        functools.partial(_mlstm_scan_body, hps=hps),
        grid=(batch * groups, nc),
        in_specs=f_in + b_in,
        out_specs=[f_out, b_out],
        out_shape=[out_sds, out_sds],
        scratch_shapes=state * (2 * hps),
        compiler_params=_params("parallel", "arbitrary"),
        name=name,
    )(*(operands * 2))


def _combine_body(hf_ref, hb_ref, o_pre_ref, g_ref, out_ref):
    h = hf_ref[...] + hb_ref[...]
    ms = jnp.mean(h * h, axis=-1, keepdims=True)
    y = h * lax.rsqrt(ms + EPS) * g_ref[...]
    out_ref[...] = (y * jax.nn.sigmoid(o_pre_ref[...])).astype(out_ref.dtype)


def _mlstm_combine(h_fwd, h_bwd, p, head_norm, *, heads, o_col, name="mlstm_combine"):
    t = p.shape[0]
    dv = MLSTM_V_DIM
    ts = _pick(t, (512, 256, 128))
    return pl.pallas_call(
        _combine_body,
        grid=(t // ts, heads),
        in_specs=[pl.BlockSpec((ts, dv), lambda i, h: (i, h)),
                  pl.BlockSpec((ts, dv), lambda i, h: (i, h)),
                  pl.BlockSpec((ts, dv), lambda i, h: (i, o_col // dv + h)),
                  pl.BlockSpec((1, dv), lambda i, h: (0, h))],
        out_specs=pl.BlockSpec((ts, dv), lambda i, h: (i, h)),
        out_shape=jax.ShapeDtypeStruct((t, heads * dv), BF16),
        compiler_params=_params("parallel", "parallel"),
        name=name,
    )(h_fwd, h_bwd, p, head_norm.reshape(1, heads * dv).astype(F32))


def _swap_halves(w, axis):
    half = w.shape[axis] // 2
    lo, hi = jnp.split(w, [half], axis=axis)
    return jnp.concatenate([hi, lo], axis=axis)


def _prep_mla_weights(w_in_t, w_uq, q_rank, kv_rank, heads):
    a1 = q_rank + kv_rank
    a2 = a1 + MLA_ROPE_DIM
    w_kr = w_in_t[a1:a2]
    w_in_p = jnp.concatenate([w_in_t[:a1], w_in_t[a2:], w_kr, _swap_halves(w_kr, 0)], axis=0).astype(BF16)
    wq = w_uq.reshape(q_rank, heads, MLA_NOPE_DIM + MLA_ROPE_DIM)
    rope_cols = wq[..., MLA_NOPE_DIM:]
    w_uq_p = jnp.concatenate([wq[..., :MLA_NOPE_DIM], rope_cols, _swap_halves(rope_cols, 2)], axis=-1)
    return w_in_p, w_uq_p.reshape(q_rank, heads * MLA_HEAD_COLS).astype(BF16)


def _rope_tables(positions):
    inv_freq = 1.0 / (ROPE_THETA ** (jnp.arange(0, MLA_ROPE_DIM, 2, dtype=F32) / MLA_ROPE_DIM))
    ang = positions.astype(F32).reshape(-1, 1) * inv_freq
    cos, sin = jnp.cos(ang), jnp.sin(ang)
    k_tab = jnp.concatenate([cos, cos, -sin, sin], axis=1)
    scale = (MLA_NOPE_DIM + MLA_ROPE_DIM) ** -0.5 * math.log2(math.e)
    q_tab = jnp.concatenate([jnp.ones((ang.shape[0], MLA_NOPE_DIM), F32), k_tab], axis=1) * scale
    return q_tab, k_tab


def kernel(x, positions, mem, attn_norm, ffn_norm, mem_norm, final_norm, mla_w_in, mla_q_norm, mla_kv_norm,
           mla_w_uq, mla_w_ukv, mlstm_w_in, mlstm_conv_w, mlstm_gate_b, mlstm_head_norm, w_mem_kv, w_out,
           w_gu, w_down):
    batch, s_len, d_model = x.shape
    depth = attn_norm.shape[0]
    n_mem = mem.shape[1]
    t = batch * s_len
    mem_width = d_model // 4
    main_width = d_model - mem_width
    mla_heads = main_width // MLA_V_DIM
    mlstm_heads = main_width // MLSTM_V_DIM
    qk_width = mlstm_heads * MLSTM_QK_DIM
    v_width = mlstm_heads * MLSTM_V_DIM
    q_rank = mla_q_norm.shape[1]
    kv_rank = mla_kv_norm.shape[1]
    d_ff = w_down.shape[1]

    tm = _pick(t, (1024, 512, 256, 128))
    q_tab, k_tab = _rope_tables(positions)
    mem_n = _rmsnorm(mem.reshape(batch * n_mem, d_model), mem_norm, width=d_model, name="mem_norm")
    w_down_bf16 = w_down.astype(BF16)
    mla_w_in_t = jnp.swapaxes(mla_w_in, 1, 2)
    mlstm_w_in_t = jnp.swapaxes(mlstm_w_in, 1, 2)
    h = x.reshape(t, d_model)

    for i in range(depth):
        j = i // N_MIXERS
        hn = _rmsnorm(h, attn_norm[i], width=d_model, name="attn_norm")
        mkv = _matmul(mem_n, w_mem_kv, layer=i, out_dtype=BF16, tm=_pick(batch * n_mem, (512, 256, 128)),
                      tn=_pick(2 * mem_width, (512, 256)), name="mem_kv")
        if i % N_MIXERS == 0:
            w_in_p, w_uq_p = _prep_mla_weights(mla_w_in_t[j], mla_w_uq[j], q_rank, kv_rank, mla_heads)
            n_in = w_in_p.shape[0]
            p = _matmul(hn, w_in_p, out_dtype=F32, tm=tm, tn=_pick(n_in, (896, 768, 640, 512, 384, 256, 128)),
                        name="mla_in_proj", transposed=True, ring=True)
            cq = _rmsnorm(p, mla_q_norm[j], width=q_rank, col_block=0, name="mla_q_norm")
            ckv = _rmsnorm(p, mla_kv_norm[j], width=kv_rank, col_block=q_rank // kv_rank, name="mla_kv_norm")
            head_cols = mla_heads * MLA_HEAD_COLS
            q = _matmul_mul(cq, w_uq_p, q_tab, out_dtype=BF16, tm=tm, tn=_pick(head_cols, (1024, 768, 512, 256)),
                            name="mla_q_proj")
            kv = _matmul(ckv, mla_w_ukv, layer=j, out_dtype=BF16, tm=tm, tn=_pick(head_cols, (1024, 768, 512, 256)),
                         name="mla_kv_proj")
            q_mem_src, q_mem_col = p, q_rank + kv_rank
            kr = _krope(p, k_tab, col_block=(q_mem_col + mem_width) // (2 * MLA_ROPE_DIM))
            main = _flash_attention(q, kv, kr, batch=batch, heads=mla_heads)
        else:
            g_col = 2 * qk_width + 2 * v_width
            w_gates_t = mlstm_w_in_t[j, g_col:g_col + 4 * mlstm_heads]
            w_q_mem_t = mlstm_w_in_t[j, g_col + 4 * mlstm_heads:]
            p = _matmul(hn, mlstm_w_in_t, layer=j, n=g_col, out_dtype=F32, tm=tm, tn=_pick(g_col, (512, 256, 128)),
                        name="mlstm_in_proj", transposed=True, ring=True)
            q_mem_src = _matmul(hn, w_q_mem_t, out_dtype=BF16, tm=tm, tn=_pick(mem_width, (512, 256, 128)),
                                name="mlstm_qmem_proj", transposed=True)
            q_mem_col = 0
            gates = _mlstm_gates(w_gates_t, hn, mlstm_gate_b[j], heads=mlstm_heads)
            gates = gates.reshape(-1, 1, MLSTM_CHUNK)
            qk = _mlstm_conv(p, mlstm_conv_w[j], batch=batch, width=2 * qk_width, q_width=qk_width)
            h_fwd, h_bwd = _mlstm_scan(qk, p, gates, batch=batch, heads=mlstm_heads, v_col=2 * qk_width)
            main = _mlstm_combine(h_fwd, h_bwd, p, mlstm_head_norm[j], heads=mlstm_heads,
                                  o_col=2 * qk_width + v_width)
        mem_out = _memory_attention(q_mem_src, mkv, batch=batch, q_col=q_mem_col, width=mem_width)
        h = _matmul_res2(main, mem_out, w_out, h, layer=i, tm=tm, tn=_pick(d_model, (512, 256)), name="out_proj",
                         ring=True)
        hn = _rmsnorm(h, ffn_norm[i], width=d_model, name="ffn_norm")
        act = _matmul_swiglu(hn, w_gu, layer=i, tm=tm, tn=_pick(d_ff, (256, 128)), name="ffn_up",
                             ring=True)
        h = _matmul_res(act, w_down_bf16, h, layer=i, tm=_pick(t, (512, 256, 128)), tn=_pick(d_model, (512, 256)),
                        name="ffn_down")
    out = _rmsnorm(h, final_norm, width=d_model, out_dtype=x.dtype, name="final_norm")
    return out.reshape(batch, s_len, d_model)
```

```python
import functools
import math

import jax
import jax.numpy as jnp
from jax import lax
from jax.experimental import pallas as pl
from jax.experimental.pallas import tpu as pltpu

F32 = jnp.float32
BF16 = jnp.bfloat16

EPS = 1e-6
ROPE_THETA = 10000.0
MLA_NOPE_DIM = 128
MLA_ROPE_DIM = 64
MLA_V_DIM = 128
MLA_HEAD_COLS = 2 * MLA_NOPE_DIM
MLSTM_V_DIM = 512
MLSTM_QK_DIM = MLSTM_V_DIM // 2
MLSTM_CHUNK = 512
MEM_HEADS = 4
IGATE_CAP = 15.0
STAB_INIT = -1e30
N_MIXERS = 2

V7X_VMEM_BYTES = 64 * 1024 * 1024
VMEM_LIMIT_BYTES = V7X_VMEM_BYTES - 8 * 1024 * 1024


def _params(*semantics):
    return pltpu.CompilerParams(dimension_semantics=semantics, vmem_limit_bytes=VMEM_LIMIT_BYTES)


def _pick(n, candidates):
    for c in candidates:
        if n % c == 0:
            return c
    return n


def _dot(a, b):
    return jnp.dot(a, b, preferred_element_type=F32)


def _dot_nt(a, b):
    return lax.dot_general(a, b, (((1,), (1,)), ((), ())), preferred_element_type=F32)


def _dot_tn(a, b):
    return lax.dot_general(a, b, (((0,), (0,)), ((), ())), preferred_element_type=F32)


def _rmsnorm_body(x_ref, g_ref, o_ref):
    x = x_ref[...].astype(F32)
    ms = jnp.mean(x * x, axis=-1, keepdims=True)
    o_ref[...] = (x * lax.rsqrt(ms + EPS) * g_ref[...]).astype(o_ref.dtype)


def _rmsnorm(x, g, *, width, col_block=0, out_dtype=BF16, name="rmsnorm"):
    t = x.shape[0]
    tm = _pick(t, (256, 128, 64, 8))
    return pl.pallas_call(
        _rmsnorm_body,
        grid=(t // tm,),
        in_specs=[pl.BlockSpec((tm, width), lambda i: (i, col_block)),
                  pl.BlockSpec((1, width), lambda i: (0, 0))],
        out_specs=pl.BlockSpec((tm, width), lambda i: (i, 0)),
        out_shape=jax.ShapeDtypeStruct((t, width), out_dtype),
        compiler_params=_params("parallel"),
        name=name,
    )(x, g.reshape(1, width).astype(F32))


def _ep_plain(mm, a, w, extra, o_ref):
    o_ref[...] = mm(a[0].astype(BF16), w[0][...]).astype(o_ref.dtype)


def _ep_mul(mm, a, w, extra, o_ref):
    table = extra[0][...]
    reps = o_ref.shape[1] // table.shape[1]
    if reps > 1:
        table = jnp.concatenate([table] * reps, axis=1)
    o_ref[...] = (mm(a[0].astype(BF16), w[0][...]) * table).astype(o_ref.dtype)


def _ep_residual(mm, a, w, extra, o_ref):
    acc = mm(a[0], w[0][...])
    for a_val, w_ref in zip(a[1:], w[1:]):
        acc = acc + mm(a_val, w_ref[...])
    o_ref[...] = extra[0][...] + acc


def _ep_swiglu(mm, a, w, extra, o_ref):
    g = mm(a[0], w[0][...])
    u = mm(a[0], w[1][...])
    o_ref[...] = (g * jax.nn.sigmoid(g) * u).astype(o_ref.dtype)


RING_SLOTS = 3


def _ring_fetch(a_hbm, bufs, sems, *, tm, n_m, total):
    step = pl.program_id(0) * n_m + pl.program_id(1)

    def copy(op, st):
        row = (st % n_m) * tm
        if not isinstance(row, int):
            row = pl.multiple_of(row, tm)
        slot = st % RING_SLOTS
        return pltpu.make_async_copy(a_hbm[op].at[pl.ds(row, tm), :], bufs[op].at[slot], sems.at[op, slot])

    @pl.when(step == 0)
    def _():
        for op in range(len(a_hbm)):
            for st in range(min(RING_SLOTS - 1, total)):
                copy(op, st).start()

    @pl.when(step + (RING_SLOTS - 1) < total)
    def _():
        for op in range(len(a_hbm)):
            copy(op, step + (RING_SLOTS - 1)).start()

    for op in range(len(a_hbm)):
        copy(op, step).wait()
    slot = step % RING_SLOTS
    return [buf[slot] for buf in bufs]


def _mm_body(*refs, n_a, n_w, n_extra, epilogue, cast, transposed, ring):
    a = refs[:n_a]
    w = refs[n_a:n_a + n_w]
    extra = refs[n_a + n_w:n_a + n_w + n_extra]
    o_ref = refs[n_a + n_w + n_extra]
    scratch = refs[n_a + n_w + n_extra + 1:]
    if cast:
        w_scratch, scratch = scratch[:n_w], scratch[n_w:]

        @pl.when(pl.program_id(1) == 0)
        def _():
            for w_ref, s_ref in zip(w, w_scratch):
                s_ref[...] = w_ref[...].astype(BF16)

        w = w_scratch
    if ring is None:
        a_vals = [a_ref[...] for a_ref in a]
    else:
        a_vals = _ring_fetch(a, scratch[:n_a], scratch[n_a], **ring)
    epilogue(_dot_nt if transposed else _dot, a_vals, w, extra, o_ref)


def _mm(name, epilogue, a_ops, w_ops, extra_ops, *, m, n, tm, tn, out_dtype, transposed=False, ring=False):
    e_specs = [pl.BlockSpec(bs, im) for _, bs, im in extra_ops]
    n_m = m // tm
    ring_args = None
    ring_scratch = []
    if ring:
        a_specs = [pl.BlockSpec(memory_space=pl.ANY) for _ in a_ops]
        ring_args = dict(tm=tm, n_m=n_m, total=(n // tn) * n_m)
        ring_scratch = [pltpu.VMEM((RING_SLOTS,) + tuple(bs), arr.dtype) for arr, bs, _ in a_ops]
        ring_scratch.append(pltpu.SemaphoreType.DMA((len(a_ops), RING_SLOTS)))
    else:
        a_specs = [pl.BlockSpec(bs, im) for _, bs, im in a_ops]

    def w_spec(layer, rows, rb, co):
        if transposed:
            return pl.BlockSpec((None, tn, rows), lambda j, i: (layer, j + co, rb))
        return pl.BlockSpec((None, rows, tn), lambda j, i: (layer, rb, j + co))

    w_specs = [w_spec(layer, rows, rb, co) for _, layer, rows, rb, co in w_ops]
    cast = w_ops[0][0].dtype != BF16
    scratch = []
    if cast:
        scratch = [pltpu.VMEM((tn, rows) if transposed else (rows, tn), BF16) for _, _, rows, _, _ in w_ops]
    return pl.pallas_call(
        functools.partial(_mm_body, n_a=len(a_ops), n_w=len(w_ops), n_extra=len(extra_ops),
                          epilogue=epilogue, cast=cast, transposed=transposed, ring=ring_args),
        grid=(n // tn, n_m),
        in_specs=a_specs + w_specs + e_specs,
        out_specs=pl.BlockSpec((tm, tn), lambda j, i: (i, j)),
        out_shape=jax.ShapeDtypeStruct((m, n), out_dtype),
        scratch_shapes=scratch + ring_scratch,
        compiler_params=_params("arbitrary", "arbitrary"),
        name=name,
    )(*[op[0] for op in a_ops], *[op[0] for op in w_ops], *[op[0] for op in extra_ops])


def _rows(tm, k, col_block=0):
    return (tm, k), lambda j, i: (i, col_block)


def _as3d(w):
    return w if w.ndim == 3 else w[None]


def _matmul(a, w, *, layer=0, n=None, out_dtype, tm, tn, name, transposed=False, ring=False):
    w = _as3d(w)
    m, k = a.shape
    if n is None:
        n = w.shape[1] if transposed else w.shape[2]
    return _mm(name, _ep_plain, [(a, *_rows(tm, k))], [(w, layer, k, 0, 0)], [],
               m=m, n=n, tm=tm, tn=tn, out_dtype=out_dtype, transposed=transposed, ring=ring)


def _matmul_mul(a, w, table, *, out_dtype, tm, tn, name):
    w = _as3d(w)
    m, k = a.shape
    c = table.shape[1]
    return _mm(name, _ep_mul, [(a, *_rows(tm, k))], [(w, 0, k, 0, 0)], [(table, (tm, c), lambda j, i: (i, 0))],
               m=m, n=w.shape[2], tm=tm, tn=tn, out_dtype=out_dtype)


def _matmul_res(a, w, res, *, layer, tm, tn, name):
    m, k = a.shape
    return _mm(name, _ep_residual, [(a, *_rows(tm, k))], [(w, layer, k, 0, 0)],
               [(res, (tm, tn), lambda j, i: (i, j))], m=m, n=w.shape[2], tm=tm, tn=tn, out_dtype=F32)


def _matmul_res2(a1, a2, w, res, *, layer, tm, tn, name, ring=False):
    m, k1 = a1.shape
    k2 = a2.shape[1]
    assert k1 % k2 == 0
    return _mm(name, _ep_residual, [(a1, *_rows(tm, k1)), (a2, *_rows(tm, k2))],
               [(w, layer, k1, 0, 0), (w, layer, k2, k1 // k2, 0)],
               [(res, (tm, tn), lambda j, i: (i, j))], m=m, n=w.shape[2], tm=tm, tn=tn, out_dtype=F32, ring=ring)


def _matmul_swiglu(a, w_gu, *, layer, tm, tn, name, ring=False):
    m, k = a.shape
    f = w_gu.shape[2] // 2
    return _mm(name, _ep_swiglu, [(a, *_rows(tm, k))], [(w_gu, layer, k, 0, 0), (w_gu, layer, k, 0, f // tn)], [],
               m=m, n=f, tm=tm, tn=tn, out_dtype=BF16, ring=ring)


def _krope_body(x_ref, t_ref, o_ref):
    y = x_ref[...] * t_ref[...]
    o_ref[...] = (y + pltpu.roll(y, MLA_ROPE_DIM, 1)).astype(o_ref.dtype)


def _krope(p, table, *, col_block, name="mla_krope"):
    t = p.shape[0]
    tm = _pick(t, (1024, 512, 256, 128, 8))
    w = 2 * MLA_ROPE_DIM
    return pl.pallas_call(
        _krope_body,
        grid=(t // tm,),
        in_specs=[pl.BlockSpec((tm, w), lambda i: (i, col_block)),
                  pl.BlockSpec((tm, w), lambda i: (i, 0))],
        out_specs=pl.BlockSpec((tm, w), lambda i: (i, 0)),
        out_shape=jax.ShapeDtypeStruct((t, w), BF16),
        compiler_params=_params("parallel"),
        name=name,
    )(p, table)


def _flash_body(q_ref, kn_ref, v_ref, kr_ref, o_ref, k_scr, *, tk):
    @pl.when(pl.program_id(2) == 0)
    def _():
        k_scr[:, :MLA_NOPE_DIM] = kn_ref[...]
        k_scr[:, MLA_NOPE_DIM:] = kr_ref[...]

    q = q_ref[...]
    tq = q.shape[0]
    s_len = k_scr.shape[0]
    m = jnp.full((tq, 1), -jnp.inf, F32)
    l = jnp.zeros((tq, 1), F32)
    acc = jnp.zeros((tq, MLA_V_DIM), F32)
    for c in range(s_len // tk):
        s = _dot_nt(q, k_scr[c * tk:(c + 1) * tk, :])
        m_new = jnp.maximum(m, jnp.max(s, axis=-1, keepdims=True))
        alpha = jnp.exp2(m - m_new)
        p = jnp.exp2(s - m_new)
        l = alpha * l + jnp.sum(p, axis=-1, keepdims=True)
        acc = alpha * acc + _dot(p.astype(BF16), v_ref[c * tk:(c + 1) * tk, :])
        m = m_new
    o_ref[...] = (acc / l).astype(o_ref.dtype)


def _flash_attention(q, kv, kr, *, batch, heads, name="mla_flash"):
    t = q.shape[0]
    s_len = t // batch
    tq = _pick(s_len, (2048, 1024, 512, 256, 128))
    tk = _pick(s_len, (1024, 512, 256, 128))
    nq = s_len // tq
    return pl.pallas_call(
        functools.partial(_flash_body, tk=tk),
        grid=(batch, heads, nq),
        in_specs=[pl.BlockSpec((tq, MLA_HEAD_COLS), lambda b, h, i: (b * nq + i, h)),
                  pl.BlockSpec((s_len, MLA_NOPE_DIM), lambda b, h, i: (b, 2 * h)),
                  pl.BlockSpec((s_len, MLA_V_DIM), lambda b, h, i: (b, 2 * h + 1)),
                  pl.BlockSpec((s_len, 2 * MLA_ROPE_DIM), lambda b, h, i: (b, 0))],
        out_specs=pl.BlockSpec((tq, MLA_V_DIM), lambda b, h, i: (b * nq + i, h)),
        out_shape=jax.ShapeDtypeStruct((t, heads * MLA_V_DIM), BF16),
        scratch_shapes=[pltpu.VMEM((s_len, MLA_HEAD_COLS), BF16)],
        compiler_params=_params("parallel", "parallel", "arbitrary"),
        name=name,
    )(q, kv, kv, kr)


def _memattn_body(*refs, scale):
    q_refs = refs[:MEM_HEADS]
    mk_ref, mv_ref, o_ref = refs[MEM_HEADS:]
    dh = q_refs[0].shape[1]
    for h in range(MEM_HEADS):
        q = (q_refs[h][...].astype(F32) * scale).astype(BF16)
        s = _dot_nt(q, mk_ref[:, h * dh:(h + 1) * dh])
        p = jnp.exp(s - jnp.max(s, axis=-1, keepdims=True))
        l = jnp.sum(p, axis=-1, keepdims=True)
        o = _dot((p / l).astype(BF16), mv_ref[:, h * dh:(h + 1) * dh])
        o_ref[:, h * dh:(h + 1) * dh] = o.astype(o_ref.dtype)


def _memory_attention(p, mkv, *, batch, q_col, width, name="mem_attn"):
    t = p.shape[0]
    s_len = t // batch
    n_mem = mkv.shape[0] // batch
    dh = width // MEM_HEADS
    tq = _pick(s_len, (512, 256, 128))
    nq = s_len // tq
    q_specs = [pl.BlockSpec((tq, dh), functools.partial(lambda b, i, h: (b * nq + i, q_col // dh + h), h=h))
               for h in range(MEM_HEADS)]
    return pl.pallas_call(
        functools.partial(_memattn_body, scale=dh ** -0.5),
        grid=(batch, nq),
        in_specs=q_specs + [pl.BlockSpec((n_mem, width), lambda b, i: (b, 0)),
                            pl.BlockSpec((n_mem, width), lambda b, i: (b, 1))],
        out_specs=pl.BlockSpec((tq, width), lambda b, i: (b * nq + i, 0)),
        out_shape=jax.ShapeDtypeStruct((t, width), BF16),
        compiler_params=_params("parallel", "parallel"),
        name=name,
    )(*([p] * MEM_HEADS), mkv, mkv)


def _gates_body(w_ref, x_ref, b_ref, f_ref, o_ref):
    g = _dot_nt(w_ref[...].astype(BF16), x_ref[...]) + b_ref[...]
    igate = IGATE_CAP * jnp.tanh(g / IGATE_CAP)
    log_fgate = jnp.minimum(g, 0.0) - jnp.log1p(jnp.exp(-jnp.abs(g)))
    o_ref[...] = jnp.where(f_ref[...] > 0.5, log_fgate, igate)


def _mlstm_gates(w_t, hn, bias, *, heads, name="mlstm_gates"):
    rows, k = w_t.shape
    t = hn.shape[0]
    tn = _pick(t, (1024, 512, 256, 128))
    is_f = (jnp.arange(rows) // heads) % 2 == 1
    return pl.pallas_call(
        _gates_body,
        grid=(t // tn,),
        in_specs=[pl.BlockSpec((rows, k), lambda i: (0, 0)),
                  pl.BlockSpec((tn, k), lambda i: (i, 0)),
                  pl.BlockSpec((rows, 1), lambda i: (0, 0)),
                  pl.BlockSpec((rows, 1), lambda i: (0, 0))],
        out_specs=pl.BlockSpec((rows, tn), lambda i: (0, i)),
        out_shape=jax.ShapeDtypeStruct((rows, t), F32),
        compiler_params=_params("parallel"),
        name=name,
    )(w_t, hn, bias.reshape(rows, 1).astype(F32), is_f.astype(F32).reshape(rows, 1))


def _conv_body(x_ref, prev_ref, next_ref, w_ref, o_ref, *, blocks_per_seq, q_blocks, q_scale):
    i = pl.program_id(0)
    j = pl.program_id(1)
    x = x_ref[...]
    ts = x.shape[0]
    pos = i % blocks_per_seq
    prev_row = jnp.where(pos == 0, 0.0, prev_ref[7:8, :])
    next_row = jnp.where(pos == blocks_per_seq - 1, 0.0, next_ref[0:1, :])
    row = lax.broadcasted_iota(jnp.int32, x.shape, 0)
    x_prev = jnp.where(row == 0, prev_row, pltpu.roll(x, 1, 0))
    x_next = jnp.where(row == ts - 1, next_row, pltpu.roll(x, ts - 1, 0))
    y = x_prev * w_ref[0:1, :] + x * w_ref[1:2, :] + x_next * w_ref[2:3, :]
    y = y * jax.nn.sigmoid(y)
    scale = jnp.where(j < q_blocks, q_scale, 1.0)
    o_ref[...] = (y * scale).astype(o_ref.dtype)


def _mlstm_conv(p, conv_w, *, batch, width, q_width, name="mlstm_conv"):
    t = p.shape[0]
    s_len = t // batch
    ts = _pick(s_len, (512, 256, 128))
    tc = _pick(q_width, (512, 256, 128))
    hb = ts // 8
    return pl.pallas_call(
        functools.partial(_conv_body, blocks_per_seq=s_len // ts, q_blocks=q_width // tc,
                          q_scale=MLSTM_QK_DIM ** -0.5),
        grid=(t // ts, width // tc),
        in_specs=[pl.BlockSpec((ts, tc), lambda i, j: (i, j)),
                  pl.BlockSpec((8, tc), lambda i, j: (jnp.maximum(i * hb - 1, 0), j)),
                  pl.BlockSpec((8, tc), lambda i, j: (jnp.minimum((i + 1) * hb, t // 8 - 1), j)),
                  pl.BlockSpec((3, tc), lambda i, j: (0, j))],
        out_specs=pl.BlockSpec((ts, tc), lambda i, j: (i, j)),
        out_shape=jax.ShapeDtypeStruct((t, width), BF16),
        compiler_params=_params("parallel", "parallel"),
        name=name,
    )(p, p, p, conv_w.astype(F32))


def _mlstm_chunk(q, k, v, ig, lf, c_scr, n_scr, m_scr, *, backward):
    ln = q.shape[0]
    t_idx = lax.broadcasted_iota(jnp.int32, (ln, ln), 0)
    j_idx = lax.broadcasted_iota(jnp.int32, (ln, ln), 1)
    mask = (j_idx >= t_idx) if backward else (j_idx <= t_idx)
    eye = j_idx == t_idx
    lf_b = jnp.broadcast_to(lf, (ln, ln))
    ig_b = jnp.broadcast_to(ig, (ln, ln))
    cs_col = jnp.sum(jnp.where(mask, lf_b, 0.0), axis=1, keepdims=True)
    cs_row = jnp.sum(jnp.where(eye, cs_col, 0.0), axis=0, keepdims=True)
    ig_col = jnp.sum(jnp.where(eye, ig_b, 0.0), axis=1, keepdims=True)
    m_prev = m_scr[...]

    dmat = jnp.where(mask, cs_col - cs_row + ig, -jnp.inf)
    inter = cs_col + m_prev
    m_row = jnp.maximum(jnp.max(dmat, axis=1, keepdims=True), inter)
    s = _dot_nt(q, k) * jnp.exp(dmat - m_row)
    w_inter = jnp.exp(inter - m_row)
    num = _dot(s.astype(BF16), v) + w_inter * _dot(q, c_scr[...].astype(BF16))
    qn = jnp.sum(q.astype(F32) * n_scr[...], axis=1, keepdims=True)
    den = jnp.sum(s, axis=1, keepdims=True) + w_inter * qn
    h = num / jnp.maximum(jnp.abs(den), jnp.exp(-m_row))

    b_end = jnp.sum(lf, axis=1, keepdims=True)
    w_tok_row = b_end - cs_row + ig
    w_tok_col = b_end - cs_col + ig_col
    m_new = jnp.maximum(b_end + m_prev, jnp.max(w_tok_row, axis=1, keepdims=True))
    decay = jnp.exp(b_end + m_prev - m_new)
    wk = k.astype(F32) * jnp.exp(w_tok_col - m_new)
    c_scr[...] = decay * c_scr[...] + _dot_tn(wk.astype(BF16), v)
    n_scr[...] = decay * n_scr[...] + jnp.sum(wk, axis=0, keepdims=True)
    m_scr[...] = m_new
    return h


def _mlstm_scan_body(*refs, hps):
    n_in = 3 + 2 * hps
    dir_in = (refs[:n_in], refs[n_in:2 * n_in])
    out_refs = refs[2 * n_in:2 * n_in + 2]
    state = refs[2 * n_in + 2:]
    dk, dv = MLSTM_QK_DIM, MLSTM_V_DIM

    @pl.when(pl.program_id(1) == 0)
    def _():
        for idx in range(0, len(state), 3):
            c_scr, n_scr, m_scr = state[idx:idx + 3]
            c_scr[...] = jnp.zeros_like(c_scr)
            n_scr[...] = jnp.zeros_like(n_scr)
            m_scr[...] = jnp.full_like(m_scr, STAB_INIT)

    for d in range(2):
        q_ref, k_ref, v_ref = dir_in[d][:3]
        for hh in range(hps):
            ig_ref, lf_ref = dir_in[d][3 + 2 * hh:5 + 2 * hh]
            chain = 3 * (2 * hh + d)
            out_refs[d][:, hh * dv:(hh + 1) * dv] = _mlstm_chunk(
                q_ref[:, hh * dk:(hh + 1) * dk], k_ref[:, hh * dk:(hh + 1) * dk],
                v_ref[:, hh * dv:(hh + 1) * dv].astype(BF16), ig_ref[0], lf_ref[0],
                *state[chain:chain + 3], backward=bool(d))


def _mlstm_scan(qk, p, gates, *, batch, heads, v_col, name="mlstm_scan"):
    t = qk.shape[0]
    s_len = t // batch
    ln = gates.shape[-1]
    nc = s_len // ln
    dk, dv = MLSTM_QK_DIM, MLSTM_V_DIM
    hps = 2 if heads % 2 == 0 else 1
    groups = heads // hps

    def specs(d):
        def chunk(c):
            return c + d * (nc - 1 - 2 * c)

        def row_block(g, c):
            return (g // groups) * nc + chunk(c)

        def gate_block(kind, hh):
            def index(g, c):
                head = (g % groups) * hps + hh
                return (((2 * d + kind) * heads + head) * batch + g // groups) * nc + chunk(c), 0, 0
            return index

        ins = [pl.BlockSpec((ln, hps * dk), lambda g, c: (row_block(g, c), g % groups)),
               pl.BlockSpec((ln, hps * dk), lambda g, c: (row_block(g, c), groups + g % groups)),
               pl.BlockSpec((ln, hps * dv), lambda g, c: (row_block(g, c), v_col // (hps * dv) + g % groups))]
        for hh in range(hps):
            ins += [pl.BlockSpec((1, 1, ln), gate_block(0, hh)), pl.BlockSpec((1, 1, ln), gate_block(1, hh))]
        out = pl.BlockSpec((ln, hps * dv), lambda g, c: (row_block(g, c), g % groups))
        return ins, out

    f_in, f_out = specs(0)
    b_in, b_out = specs(1)
    state = [pltpu.VMEM((dk, dv), F32), pltpu.VMEM((1, dk), F32), pltpu.VMEM((1, 1), F32)]
    out_sds = jax.ShapeDtypeStruct((t, heads * dv), F32)
    operands = [qk, qk, p] + [gates, gates] * hps
    return pl.pallas_call(
        functools.partial(_mlstm_scan_body, hps=hps),
        grid=(batch * groups, nc),
        in_specs=f_in + b_in,
        out_specs=[f_out, b_out],
        out_shape=[out_sds, out_sds],
        scratch_shapes=state * (2 * hps),
        compiler_params=_params("parallel", "arbitrary"),
        name=name,
    )(*(operands * 2))


def _combine_body(hf_ref, hb_ref, o_pre_ref, g_ref, out_ref):
    h = hf_ref[...] + hb_ref[...]
    ms = jnp.mean(h * h, axis=-1, keepdims=True)
    y = h * lax.rsqrt(ms + EPS) * g_ref[...]
    out_ref[...] = (y * jax.nn.sigmoid(o_pre_ref[...])).astype(out_ref.dtype)


def _mlstm_combine(h_fwd, h_bwd, p, head_norm, *, heads, o_col, name="mlstm_combine"):
    t = p.shape[0]
    dv = MLSTM_V_DIM
    ts = _pick(t, (512, 256, 128))
    return pl.pallas_call(
        _combine_body,
        grid=(t // ts, heads),
        in_specs=[pl.BlockSpec((ts, dv), lambda i, h: (i, h)),
                  pl.BlockSpec((ts, dv), lambda i, h: (i, h)),
                  pl.BlockSpec((ts, dv), lambda i, h: (i, o_col // dv + h)),
                  pl.BlockSpec((1, dv), lambda i, h: (0, h))],
        out_specs=pl.BlockSpec((ts, dv), lambda i, h: (i, h)),
        out_shape=jax.ShapeDtypeStruct((t, heads * dv), BF16),
        compiler_params=_params("parallel", "parallel"),
        name=name,
    )(h_fwd, h_bwd, p, head_norm.reshape(1, heads * dv).astype(F32))


def _swap_halves(w, axis):
    half = w.shape[axis] // 2
    lo, hi = jnp.split(w, [half], axis=axis)
    return jnp.concatenate([hi, lo], axis=axis)


def _prep_mla_weights(w_in_t, w_uq, q_rank, kv_rank, heads):
    a1 = q_rank + kv_rank
    a2 = a1 + MLA_ROPE_DIM
    w_kr = w_in_t[a1:a2]
    w_in_p = jnp.concatenate([w_in_t[:a1], w_in_t[a2:], w_kr, _swap_halves(w_kr, 0)], axis=0).astype(BF16)
    wq = w_uq.reshape(q_rank, heads, MLA_NOPE_DIM + MLA_ROPE_DIM)
    rope_cols = wq[..., MLA_NOPE_DIM:]
    w_uq_p = jnp.concatenate([wq[..., :MLA_NOPE_DIM], rope_cols, _swap_halves(rope_cols, 2)], axis=-1)
    return w_in_p, w_uq_p.reshape(q_rank, heads * MLA_HEAD_COLS).astype(BF16)


def _rope_tables(positions):
    inv_freq = 1.0 / (ROPE_THETA ** (jnp.arange(0, MLA_ROPE_DIM, 2, dtype=F32) / MLA_ROPE_DIM))
    ang = positions.astype(F32).reshape(-1, 1) * inv_freq
    cos, sin = jnp.cos(ang), jnp.sin(ang)
    k_tab = jnp.concatenate([cos, cos, -sin, sin], axis=1)
    scale = (MLA_NOPE_DIM + MLA_ROPE_DIM) ** -0.5 * math.log2(math.e)
    q_tab = jnp.concatenate([jnp.ones((ang.shape[0], MLA_NOPE_DIM), F32), k_tab], axis=1) * scale
    return q_tab, k_tab


def kernel(x, positions, mem, attn_norm, ffn_norm, mem_norm, final_norm, mla_w_in, mla_q_norm, mla_kv_norm,
           mla_w_uq, mla_w_ukv, mlstm_w_in, mlstm_conv_w, mlstm_gate_b, mlstm_head_norm, w_mem_kv, w_out,
           w_gu, w_down):
    batch, s_len, d_model = x.shape
    depth = attn_norm.shape[0]
    n_mem = mem.shape[1]
    t = batch * s_len
    mem_width = d_model // 4
    main_width = d_model - mem_width
    mla_heads = main_width // MLA_V_DIM
    mlstm_heads = main_width // MLSTM_V_DIM
    qk_width = mlstm_heads * MLSTM_QK_DIM
    v_width = mlstm_heads * MLSTM_V_DIM
    q_rank = mla_q_norm.shape[1]
    kv_rank = mla_kv_norm.shape[1]
    d_ff = w_down.shape[1]

    tm = _pick(t, (1024, 512, 256, 128))
    q_tab, k_tab = _rope_tables(positions)
    mem_n = _rmsnorm(mem.reshape(batch * n_mem, d_model), mem_norm, width=d_model, name="mem_norm")
    w_down_bf16 = w_down.astype(BF16)
    mla_w_in_t = jnp.swapaxes(mla_w_in, 1, 2)
    mlstm_w_in_t = jnp.swapaxes(mlstm_w_in, 1, 2)
    h = x.reshape(t, d_model)

    for i in range(depth):
        j = i // N_MIXERS
        hn = _rmsnorm(h, attn_norm[i], width=d_model, name="attn_norm")
        mkv = _matmul(mem_n, w_mem_kv, layer=i, out_dtype=BF16, tm=_pick(batch * n_mem, (512, 256, 128)),
                      tn=_pick(2 * mem_width, (512, 256)), name="mem_kv")
        if i % N_MIXERS == 0:
            w_in_p, w_uq_p = _prep_mla_weights(mla_w_in_t[j], mla_w_uq[j], q_rank, kv_rank, mla_heads)
            n_in = w_in_p.shape[0]
            p = _matmul(hn, w_in_p, out_dtype=F32, tm=tm, tn=_pick(n_in, (896, 768, 640, 512, 384, 256, 128)),
                        name="mla_in_proj", transposed=True, ring=True)
            cq = _rmsnorm(p, mla_q_norm[j], width=q_rank, col_block=0, name="mla_q_norm")
            ckv = _rmsnorm(p, mla_kv_norm[j], width=kv_rank, col_block=q_rank // kv_rank, name="mla_kv_norm")
            head_cols = mla_heads * MLA_HEAD_COLS
            tm_small_k = _pick(t, (2048, 1024, 512, 256, 128))
            q = _matmul_mul(cq, w_uq_p, q_tab, out_dtype=BF16, tm=tm_small_k,
                            tn=_pick(head_cols, (1024, 768, 512, 256)), name="mla_q_proj")
            kv = _matmul(ckv, mla_w_ukv, layer=j, out_dtype=BF16, tm=tm_small_k,
                         tn=_pick(head_cols, (1536, 1024, 768, 512, 256)), name="mla_kv_proj")
            q_mem_src, q_mem_col = p, q_rank + kv_rank
            kr = _krope(p, k_tab, col_block=(q_mem_col + mem_width) // (2 * MLA_ROPE_DIM))
            main = _flash_attention(q, kv, kr, batch=batch, heads=mla_heads)
        else:
            g_col = 2 * qk_width + 2 * v_width
            w_gates_t = mlstm_w_in_t[j, g_col:g_col + 4 * mlstm_heads]
            w_q_mem_t = mlstm_w_in_t[j, g_col + 4 * mlstm_heads:]
            p = _matmul(hn, mlstm_w_in_t, layer=j, n=g_col, out_dtype=F32, tm=tm, tn=_pick(g_col, (512, 256, 128)),
                        name="mlstm_in_proj", transposed=True, ring=True)
            q_mem_src = _matmul(hn, w_q_mem_t, out_dtype=BF16, tm=tm, tn=_pick(mem_width, (512, 256, 128)),
                                name="mlstm_qmem_proj", transposed=True)
            q_mem_col = 0
            gates = _mlstm_gates(w_gates_t, hn, mlstm_gate_b[j], heads=mlstm_heads)
            gates = gates.reshape(-1, 1, min(MLSTM_CHUNK, s_len))
            qk = _mlstm_conv(p, mlstm_conv_w[j], batch=batch, width=2 * qk_width, q_width=qk_width)
            h_fwd, h_bwd = _mlstm_scan(qk, p, gates, batch=batch, heads=mlstm_heads, v_col=2 * qk_width)
            main = _mlstm_combine(h_fwd, h_bwd, p, mlstm_head_norm[j], heads=mlstm_heads,
                                  o_col=2 * qk_width + v_width)
        mem_out = _memory_attention(q_mem_src, mkv, batch=batch, q_col=q_mem_col, width=mem_width)
        h = _matmul_res2(main, mem_out, w_out, h, layer=i, tm=tm, tn=_pick(d_model, (512, 256)), name="out_proj",
                         ring=True)
        hn = _rmsnorm(h, ffn_norm[i], width=d_model, name="ffn_norm")
        act = _matmul_swiglu(hn, w_gu, layer=i, tm=tm, tn=_pick(d_ff, (256, 128)), name="ffn_up",
                             ring=True)
        h = _matmul_res(act, w_down_bf16, h, layer=i, tm=_pick(t, (512, 256, 128)), tn=_pick(d_model, (512, 256)),
                        name="ffn_down")
    out = _rmsnorm(h, final_norm, width=d_model, out_dtype=x.dtype, name="final_norm")
    return out.reshape(batch, s_len, d_model)
```

```python
import functools
import math

import jax
import jax.numpy as jnp
from jax import lax
from jax.experimental import pallas as pl
from jax.experimental.pallas import tpu as pltpu

F32 = jnp.float32
BF16 = jnp.bfloat16

EPS = 1e-6
ROPE_THETA = 10000.0
MLA_NOPE_DIM = 128
MLA_ROPE_DIM = 64
MLA_V_DIM = 128
MLA_HEAD_COLS = 2 * MLA_NOPE_DIM
MLSTM_V_DIM = 512
MLSTM_QK_DIM = MLSTM_V_DIM // 2
MLSTM_CHUNK = 512
MEM_HEADS = 4
IGATE_CAP = 15.0
STAB_INIT = -1e30
N_MIXERS = 2

V7X_VMEM_BYTES = 64 * 1024 * 1024
VMEM_LIMIT_BYTES = V7X_VMEM_BYTES - 8 * 1024 * 1024


def _params(*semantics):
    return pltpu.CompilerParams(dimension_semantics=semantics, vmem_limit_bytes=VMEM_LIMIT_BYTES)


def _pick(n, candidates):
    for c in candidates:
        if n % c == 0:
            return c
    return n


def _dot(a, b):
    return jnp.dot(a, b, preferred_element_type=F32)


def _dot_nt(a, b):
    return lax.dot_general(a, b, (((1,), (1,)), ((), ())), preferred_element_type=F32)


def _dot_tn(a, b):
    return lax.dot_general(a, b, (((0,), (0,)), ((), ())), preferred_element_type=F32)


def _rmsnorm_body(x_ref, g_ref, o_ref):
    x = x_ref[...].astype(F32)
    ms = jnp.mean(x * x, axis=-1, keepdims=True)
    o_ref[...] = (x * lax.rsqrt(ms + EPS) * g_ref[...]).astype(o_ref.dtype)


def _rmsnorm(x, g, *, width, col_block=0, out_dtype=BF16, name="rmsnorm"):
    t = x.shape[0]
    tm = _pick(t, (256, 128, 64, 8))
    return pl.pallas_call(
        _rmsnorm_body,
        grid=(t // tm,),
        in_specs=[pl.BlockSpec((tm, width), lambda i: (i, col_block)),
                  pl.BlockSpec((1, width), lambda i: (0, 0))],
        out_specs=pl.BlockSpec((tm, width), lambda i: (i, 0)),
        out_shape=jax.ShapeDtypeStruct((t, width), out_dtype),
        compiler_params=_params("parallel"),
        name=name,
    )(x, g.reshape(1, width).astype(F32))


def _ep_plain(mm, a, w, extra, o_ref):
    o_ref[...] = mm(a[0].astype(BF16), w[0][...]).astype(o_ref.dtype)


def _ep_mul(mm, a, w, extra, o_ref):
    table = extra[0][...]
    reps = o_ref.shape[1] // table.shape[1]
    if reps > 1:
        table = jnp.concatenate([table] * reps, axis=1)
    o_ref[...] = (mm(a[0].astype(BF16), w[0][...]) * table).astype(o_ref.dtype)


def _ep_residual(mm, a, w, extra, o_ref):
    acc = mm(a[0], w[0][...])
    for a_val, w_ref in zip(a[1:], w[1:]):
        acc = acc + mm(a_val, w_ref[...])
    o_ref[...] = extra[0][...] + acc


def _ep_swiglu(mm, a, w, extra, o_ref):
    g = mm(a[0], w[0][...])
    u = mm(a[0], w[1][...])
    o_ref[...] = (g * jax.nn.sigmoid(g) * u).astype(o_ref.dtype)


RING_SLOTS = 3


def _ring_fetch(a_hbm, bufs, sems, *, tm, n_m, total):
    step = pl.program_id(0) * n_m + pl.program_id(1)

    def copy(op, st):
        row = (st % n_m) * tm
        if not isinstance(row, int):
            row = pl.multiple_of(row, tm)
        slot = st % RING_SLOTS
        return pltpu.make_async_copy(a_hbm[op].at[pl.ds(row, tm), :], bufs[op].at[slot], sems.at[op, slot])

    @pl.when(step == 0)
    def _():
        for op in range(len(a_hbm)):
            for st in range(min(RING_SLOTS - 1, total)):
                copy(op, st).start()

    @pl.when(step + (RING_SLOTS - 1) < total)
    def _():
        for op in range(len(a_hbm)):
            copy(op, step + (RING_SLOTS - 1)).start()

    for op in range(len(a_hbm)):
        copy(op, step).wait()
    slot = step % RING_SLOTS
    return [buf[slot] for buf in bufs]


def _mm_body(*refs, n_a, n_w, n_extra, epilogue, cast, transposed, ring, side):
    a = refs[:n_a]
    w = refs[n_a:n_a + n_w]
    extra = refs[n_a + n_w:n_a + n_w + n_extra]
    n_in = n_a + n_w + n_extra
    o_ref = refs[n_in]
    scratch = refs[n_in + 1:]
    if side:
        refs[n_in + 2][...] = refs[n_in][...].astype(BF16)
        o_ref = refs[n_in + 1]
        scratch = refs[n_in + 3:]
    if cast:
        w_scratch, scratch = scratch[:n_w], scratch[n_w:]

        @pl.when(pl.program_id(1) == 0)
        def _():
            for w_ref, s_ref in zip(w, w_scratch):
                s_ref[...] = w_ref[...].astype(BF16)

        w = w_scratch
    if ring is None:
        a_vals = [a_ref[...] for a_ref in a]
    else:
        a_vals = _ring_fetch(a, scratch[:n_a], scratch[n_a], **ring)
    epilogue(_dot_nt if transposed else _dot, a_vals, w, extra, o_ref)


def _mm(name, epilogue, a_ops, w_ops, extra_ops, *, m, n, tm, tn, out_dtype, transposed=False, ring=False,
        side=None):
    e_specs = [pl.BlockSpec(bs, im) for _, bs, im in extra_ops]
    n_m = m // tm
    out_specs = pl.BlockSpec((tm, tn), lambda j, i: (i, j))
    out_shape = jax.ShapeDtypeStruct((m, n), out_dtype)
    side_ops = []
    if side is not None:
        side_arr, side_layer = side
        _, side_rows, side_cols = side_arr.shape
        slab = side_rows // ((n // tn) * n_m)
        assert slab * (n // tn) * n_m == side_rows and slab % 16 == 0
        e_specs.append(pl.BlockSpec((None, slab, side_cols), lambda j, i: (side_layer, j * n_m + i, 0)))
        out_specs = [out_specs, pl.BlockSpec((slab, side_cols), lambda j, i: (j * n_m + i, 0))]
        out_shape = [out_shape, jax.ShapeDtypeStruct((side_rows, side_cols), BF16)]
        side_ops = [side_arr]
    ring_args = None
    ring_scratch = []
    if ring:
        a_specs = [pl.BlockSpec(memory_space=pl.ANY) for _ in a_ops]
        ring_args = dict(tm=tm, n_m=n_m, total=(n // tn) * n_m)
        ring_scratch = [pltpu.VMEM((RING_SLOTS,) + tuple(bs), arr.dtype) for arr, bs, _ in a_ops]
        ring_scratch.append(pltpu.SemaphoreType.DMA((len(a_ops), RING_SLOTS)))
    else:
        a_specs = [pl.BlockSpec(bs, im) for _, bs, im in a_ops]

    def w_spec(layer, rows, rb, co):
        if transposed:
            return pl.BlockSpec((None, tn, rows), lambda j, i: (layer, j + co, rb))
        return pl.BlockSpec((None, rows, tn), lambda j, i: (layer, rb, j + co))

    w_specs = [w_spec(layer, rows, rb, co) for _, layer, rows, rb, co in w_ops]
    cast = w_ops[0][0].dtype != BF16
    scratch = []
    if cast:
        scratch = [pltpu.VMEM((tn, rows) if transposed else (rows, tn), BF16) for _, _, rows, _, _ in w_ops]
    return pl.pallas_call(
        functools.partial(_mm_body, n_a=len(a_ops), n_w=len(w_ops), n_extra=len(extra_ops),
                          epilogue=epilogue, cast=cast, transposed=transposed, ring=ring_args,
                          side=side is not None),
        grid=(n // tn, n_m),
        in_specs=a_specs + w_specs + e_specs,
        out_specs=out_specs,
        out_shape=out_shape,
        scratch_shapes=scratch + ring_scratch,
        compiler_params=_params("arbitrary", "arbitrary"),
        name=name,
    )(*[op[0] for op in a_ops], *[op[0] for op in w_ops], *[op[0] for op in extra_ops], *side_ops)


def _rows(tm, k, col_block=0):
    return (tm, k), lambda j, i: (i, col_block)


def _as3d(w):
    return w if w.ndim == 3 else w[None]


def _matmul(a, w, *, layer=0, n=None, out_dtype, tm, tn, name, transposed=False, ring=False):
    w = _as3d(w)
    m, k = a.shape
    if n is None:
        n = w.shape[1] if transposed else w.shape[2]
    return _mm(name, _ep_plain, [(a, *_rows(tm, k))], [(w, layer, k, 0, 0)], [],
               m=m, n=n, tm=tm, tn=tn, out_dtype=out_dtype, transposed=transposed, ring=ring)


def _matmul_mul(a, w, table, *, out_dtype, tm, tn, name):
    w = _as3d(w)
    m, k = a.shape
    c = table.shape[1]
    return _mm(name, _ep_mul, [(a, *_rows(tm, k))], [(w, 0, k, 0, 0)], [(table, (tm, c), lambda j, i: (i, 0))],
               m=m, n=w.shape[2], tm=tm, tn=tn, out_dtype=out_dtype)


def _matmul_res(a, w, res, *, layer, tm, tn, name):
    m, k = a.shape
    return _mm(name, _ep_residual, [(a, *_rows(tm, k))], [(w, layer, k, 0, 0)],
               [(res, (tm, tn), lambda j, i: (i, j))], m=m, n=w.shape[2], tm=tm, tn=tn, out_dtype=F32)


def _matmul_res2(a1, a2, w, res, *, layer, tm, tn, name, ring=False):
    m, k1 = a1.shape
    k2 = a2.shape[1]
    assert k1 % k2 == 0
    return _mm(name, _ep_residual, [(a1, *_rows(tm, k1)), (a2, *_rows(tm, k2))],
               [(w, layer, k1, 0, 0), (w, layer, k2, k1 // k2, 0)],
               [(res, (tm, tn), lambda j, i: (i, j))], m=m, n=w.shape[2], tm=tm, tn=tn, out_dtype=F32, ring=ring)


def _matmul_swiglu(a, w_gu, *, layer, tm, tn, name, ring=False, side=None):
    m, k = a.shape
    f = w_gu.shape[2] // 2
    return _mm(name, _ep_swiglu, [(a, *_rows(tm, k))], [(w_gu, layer, k, 0, 0), (w_gu, layer, k, 0, f // tn)], [],
               m=m, n=f, tm=tm, tn=tn, out_dtype=BF16, ring=ring, side=side)


def _krope_body(x_ref, t_ref, o_ref):
    y = x_ref[...] * t_ref[...]
    o_ref[...] = (y + pltpu.roll(y, MLA_ROPE_DIM, 1)).astype(o_ref.dtype)


def _krope(p, table, *, col_block, name="mla_krope"):
    t = p.shape[0]
    tm = _pick(t, (1024, 512, 256, 128, 8))
    w = 2 * MLA_ROPE_DIM
    return pl.pallas_call(
        _krope_body,
        grid=(t // tm,),
        in_specs=[pl.BlockSpec((tm, w), lambda i: (i, col_block)),
                  pl.BlockSpec((tm, w), lambda i: (i, 0))],
        out_specs=pl.BlockSpec((tm, w), lambda i: (i, 0)),
        out_shape=jax.ShapeDtypeStruct((t, w), BF16),
        compiler_params=_params("parallel"),
        name=name,
    )(p, table)


def _flash_body(q_ref, kn_ref, v_ref, kr_ref, o_ref, k_scr, *, tk):
    @pl.when(pl.program_id(2) == 0)
    def _():
        k_scr[:, :MLA_NOPE_DIM] = kn_ref[...]
        k_scr[:, MLA_NOPE_DIM:] = kr_ref[...]

    q = q_ref[...]
    tq = q.shape[0]
    s_len = k_scr.shape[0]
    m = jnp.full((tq, 1), -jnp.inf, F32)
    l = jnp.zeros((tq, 1), F32)
    acc = jnp.zeros((tq, MLA_V_DIM), F32)
    for c in range(s_len // tk):
        s = _dot_nt(q, k_scr[c * tk:(c + 1) * tk, :])
        m_new = jnp.maximum(m, jnp.max(s, axis=-1, keepdims=True))
        alpha = jnp.exp2(m - m_new)
        p = jnp.exp2(s - m_new)
        l = alpha * l + jnp.sum(p, axis=-1, keepdims=True)
        acc = alpha * acc + _dot(p.astype(BF16), v_ref[c * tk:(c + 1) * tk, :])
        m = m_new
    o_ref[...] = (acc / l).astype(o_ref.dtype)


def _flash_attention(q, kv, kr, *, batch, heads, name="mla_flash"):
    t = q.shape[0]
    s_len = t // batch
    tq = _pick(s_len, (2048, 1024, 512, 256, 128))
    tk = _pick(s_len, (1024, 512, 256, 128))
    nq = s_len // tq
    return pl.pallas_call(
        functools.partial(_flash_body, tk=tk),
        grid=(batch, heads, nq),
        in_specs=[pl.BlockSpec((tq, MLA_HEAD_COLS), lambda b, h, i: (b * nq + i, h)),
                  pl.BlockSpec((s_len, MLA_NOPE_DIM), lambda b, h, i: (b, 2 * h)),
                  pl.BlockSpec((s_len, MLA_V_DIM), lambda b, h, i: (b, 2 * h + 1)),
                  pl.BlockSpec((s_len, 2 * MLA_ROPE_DIM), lambda b, h, i: (b, 0))],
        out_specs=pl.BlockSpec((tq, MLA_V_DIM), lambda b, h, i: (b * nq + i, h)),
        out_shape=jax.ShapeDtypeStruct((t, heads * MLA_V_DIM), BF16),
        scratch_shapes=[pltpu.VMEM((s_len, MLA_HEAD_COLS), BF16)],
        compiler_params=_params("parallel", "parallel", "arbitrary"),
        name=name,
    )(q, kv, kv, kr)


def _memattn_body(*refs, scale):
    q_refs = refs[:MEM_HEADS]
    mk_ref, mv_ref, o_ref = refs[MEM_HEADS:]
    dh = q_refs[0].shape[1]
    for h in range(MEM_HEADS):
        q = (q_refs[h][...].astype(F32) * scale).astype(BF16)
        s = _dot_nt(q, mk_ref[:, h * dh:(h + 1) * dh])
        p = jnp.exp(s - jnp.max(s, axis=-1, keepdims=True))
        l = jnp.sum(p, axis=-1, keepdims=True)
        o = _dot((p / l).astype(BF16), mv_ref[:, h * dh:(h + 1) * dh])
        o_ref[:, h * dh:(h + 1) * dh] = o.astype(o_ref.dtype)


def _memory_attention(p, mkv, *, batch, q_col, width, name="mem_attn"):
    t = p.shape[0]
    s_len = t // batch
    n_mem = mkv.shape[0] // batch
    dh = width // MEM_HEADS
    tq = _pick(s_len, (512, 256, 128))
    nq = s_len // tq
    q_specs = [pl.BlockSpec((tq, dh), functools.partial(lambda b, i, h: (b * nq + i, q_col // dh + h), h=h))
               for h in range(MEM_HEADS)]
    return pl.pallas_call(
        functools.partial(_memattn_body, scale=dh ** -0.5),
        grid=(batch, nq),
        in_specs=q_specs + [pl.BlockSpec((n_mem, width), lambda b, i: (b, 0)),
                            pl.BlockSpec((n_mem, width), lambda b, i: (b, 1))],
        out_specs=pl.BlockSpec((tq, width), lambda b, i: (b * nq + i, 0)),
        out_shape=jax.ShapeDtypeStruct((t, width), BF16),
        compiler_params=_params("parallel", "parallel"),
        name=name,
    )(*([p] * MEM_HEADS), mkv, mkv)


def _gates_body(w_ref, x_ref, b_ref, f_ref, o_ref):
    g = _dot_nt(w_ref[...].astype(BF16), x_ref[...]) + b_ref[...]
    igate = IGATE_CAP * jnp.tanh(g / IGATE_CAP)
    log_fgate = jnp.minimum(g, 0.0) - jnp.log1p(jnp.exp(-jnp.abs(g)))
    o_ref[...] = jnp.where(f_ref[...] > 0.5, log_fgate, igate)


def _mlstm_gates(w_t, hn, bias, *, heads, name="mlstm_gates"):
    rows, k = w_t.shape
    t = hn.shape[0]
    tn = _pick(t, (1024, 512, 256, 128))
    is_f = (jnp.arange(rows) // heads) % 2 == 1
    return pl.pallas_call(
        _gates_body,
        grid=(t // tn,),
        in_specs=[pl.BlockSpec((rows, k), lambda i: (0, 0)),
                  pl.BlockSpec((tn, k), lambda i: (i, 0)),
                  pl.BlockSpec((rows, 1), lambda i: (0, 0)),
                  pl.BlockSpec((rows, 1), lambda i: (0, 0))],
        out_specs=pl.BlockSpec((rows, tn), lambda i: (0, i)),
        out_shape=jax.ShapeDtypeStruct((rows, t), F32),
        compiler_params=_params("parallel"),
        name=name,
    )(w_t, hn, bias.reshape(rows, 1).astype(F32), is_f.astype(F32).reshape(rows, 1))


def _conv_body(x_ref, prev_ref, next_ref, w_ref, o_ref, *, blocks_per_seq, q_blocks, q_scale):
    i = pl.program_id(0)
    j = pl.program_id(1)
    x = x_ref[...].astype(F32)
    ts = x.shape[0]
    pos = i % blocks_per_seq
    halo = prev_ref.shape[0]
    prev_row = jnp.where(pos == 0, 0.0, prev_ref[...].astype(F32)[halo - 1:halo, :])
    next_row = jnp.where(pos == blocks_per_seq - 1, 0.0, next_ref[...].astype(F32)[0:1, :])
    row = lax.broadcasted_iota(jnp.int32, x.shape, 0)
    x_prev = jnp.where(row == 0, prev_row, pltpu.roll(x, 1, 0))
    x_next = jnp.where(row == ts - 1, next_row, pltpu.roll(x, ts - 1, 0))
    y = x_prev * w_ref[0:1, :] + x * w_ref[1:2, :] + x_next * w_ref[2:3, :]
    y = y * jax.nn.sigmoid(y)
    scale = jnp.where(j < q_blocks, q_scale, 1.0)
    o_ref[...] = (y * scale).astype(o_ref.dtype)


def _mlstm_conv(p, conv_w, *, batch, width, q_width, name="mlstm_conv"):
    t = p.shape[0]
    s_len = t // batch
    ts = _pick(s_len, (512, 256, 128))
    tc = _pick(q_width, (512, 256, 128))
    halo = 8 * (4 // p.dtype.itemsize)
    hb = ts // halo
    return pl.pallas_call(
        functools.partial(_conv_body, blocks_per_seq=s_len // ts, q_blocks=q_width // tc,
                          q_scale=MLSTM_QK_DIM ** -0.5),
        grid=(t // ts, width // tc),
        in_specs=[pl.BlockSpec((ts, tc), lambda i, j: (i, j)),
                  pl.BlockSpec((halo, tc), lambda i, j: (jnp.maximum(i * hb - 1, 0), j)),
                  pl.BlockSpec((halo, tc), lambda i, j: (jnp.minimum((i + 1) * hb, t // halo - 1), j)),
                  pl.BlockSpec((3, tc), lambda i, j: (0, j))],
        out_specs=pl.BlockSpec((ts, tc), lambda i, j: (i, j)),
        out_shape=jax.ShapeDtypeStruct((t, width), BF16),
        compiler_params=_params("parallel", "parallel"),
        name=name,
    )(p, p, p, conv_w.astype(F32))


def _mlstm_chunk(q, k, v, ig, lf, c_scr, n_scr, m_scr, *, sign):
    ln = q.shape[0]
    t_idx = lax.broadcasted_iota(jnp.int32, (ln, ln), 0)
    j_idx = lax.broadcasted_iota(jnp.int32, (ln, ln), 1)
    mask = (j_idx - t_idx) * sign <= 0
    eye = j_idx == t_idx
    lf_b = jnp.broadcast_to(lf, (ln, ln))
    ig_b = jnp.broadcast_to(ig, (ln, ln))
    cs_col = jnp.sum(jnp.where(mask, lf_b, 0.0), axis=1, keepdims=True)
    cs_row = jnp.sum(jnp.where(eye, cs_col, 0.0), axis=0, keepdims=True)
    ig_col = jnp.sum(jnp.where(eye, ig_b, 0.0), axis=1, keepdims=True)
    m_prev = m_scr[...]

    dmat = jnp.where(mask, cs_col - cs_row + ig, -jnp.inf)
    inter = cs_col + m_prev
    m_row = jnp.maximum(jnp.max(dmat, axis=1, keepdims=True), inter)
    s = _dot_nt(q, k) * jnp.exp(dmat - m_row)
    w_inter = jnp.exp(inter - m_row)
    num = _dot(s.astype(BF16), v) + w_inter * _dot(q, c_scr[...].astype(BF16))
    qn = jnp.sum(q.astype(F32) * n_scr[...], axis=1, keepdims=True)
    den = jnp.sum(s, axis=1, keepdims=True) + w_inter * qn
    h = num / jnp.maximum(jnp.abs(den), jnp.exp(-m_row))

    b_end = jnp.sum(lf, axis=1, keepdims=True)
    w_tok_row = b_end - cs_row + ig
    w_tok_col = b_end - cs_col + ig_col
    m_new = jnp.maximum(b_end + m_prev, jnp.max(w_tok_row, axis=1, keepdims=True))
    decay = jnp.exp(b_end + m_prev - m_new)
    wk = k.astype(F32) * jnp.exp(w_tok_col - m_new)
    c_scr[...] = decay * c_scr[...] + _dot_tn(wk.astype(BF16), v)
    n_scr[...] = decay * n_scr[...] + jnp.sum(wk, axis=0, keepdims=True)
    m_scr[...] = m_new
    return h


def _mlstm_scan_body(*refs, hps, nc):
    q_ref, k_ref, v_ref = refs[:3]
    gate_refs = refs[3:3 + 2 * hps]
    o_pre_ref, g_ref, out_ref = refs[3 + 2 * hps:6 + 2 * hps]
    hf_scr = refs[6 + 2 * hps]
    state = refs[7 + 2 * hps:]
    dk, dv = MLSTM_QK_DIM, MLSTM_V_DIM
    ln = q_ref.shape[0]
    d = pl.program_id(1)
    c = pl.program_id(2)

    @pl.when(c == 0)
    def _():
        for idx in range(0, len(state), 3):
            c_scr, n_scr, m_scr = state[idx:idx + 3]
            c_scr[...] = jnp.zeros_like(c_scr)
            n_scr[...] = jnp.zeros_like(n_scr)
            m_scr[...] = jnp.full_like(m_scr, STAB_INIT)

    hs = [_mlstm_chunk(q_ref[:, hh * dk:(hh + 1) * dk], k_ref[:, hh * dk:(hh + 1) * dk],
                       v_ref[:, hh * dv:(hh + 1) * dv].astype(BF16), gate_refs[2 * hh][0], gate_refs[2 * hh + 1][0],
                       *state[3 * hh:3 * hh + 3], sign=1 - 2 * d) for hh in range(hps)]

    @pl.when(d == 0)
    def _():
        rows = pl.ds(pl.multiple_of(c * ln, ln), ln)
        for hh in range(hps):
            hf_scr[rows, hh * dv:(hh + 1) * dv] = hs[hh]

    @pl.when(d == 1)
    def _():
        rows = pl.ds(pl.multiple_of((nc - 1 - c) * ln, ln), ln)
        for hh in range(hps):
            cols = slice(hh * dv, (hh + 1) * dv)
            h = hf_scr[rows, cols] + hs[hh]
            ms = jnp.mean(h * h, axis=-1, keepdims=True)
            y = h * lax.rsqrt(ms + EPS) * g_ref[:, cols]
            out_ref[:, cols] = (y * jax.nn.sigmoid(o_pre_ref[:, cols].astype(F32))).astype(out_ref.dtype)


def _mlstm_scan(qk, p, gates, head_norm, *, batch, heads, v_col, o_col, name="mlstm_scan"):
    t = qk.shape[0]
    s_len = t // batch
    ln = gates.shape[-1]
    nc = s_len // ln
    dk, dv = MLSTM_QK_DIM, MLSTM_V_DIM
    hps = 2 if heads % 2 == 0 else 1
    groups = heads // hps

    def chunk(d, c):
        return c + d * (nc - 1 - 2 * c)

    def row_block(g, d, c):
        return (g // groups) * nc + chunk(d, c)

    def out_row_block(g, d, c):
        return (g // groups) * nc + nc - 1 - d * c

    def gate_block(kind, hh):
        def index(g, d, c):
            head = (g % groups) * hps + hh
            return (((2 * d + kind) * heads + head) * batch + g // groups) * nc + chunk(d, c), 0, 0
        return index

    wide = hps * dv
    in_specs = [pl.BlockSpec((ln, hps * dk), lambda g, d, c: (row_block(g, d, c), g % groups)),
                pl.BlockSpec((ln, hps * dk), lambda g, d, c: (row_block(g, d, c), groups + g % groups)),
                pl.BlockSpec((ln, wide), lambda g, d, c: (row_block(g, d, c), v_col // wide + g % groups))]
    for hh in range(hps):
        in_specs += [pl.BlockSpec((1, 1, ln), gate_block(0, hh)), pl.BlockSpec((1, 1, ln), gate_block(1, hh))]
    in_specs += [pl.BlockSpec((ln, wide), lambda g, d, c: (out_row_block(g, d, c), o_col // wide + g % groups)),
                 pl.BlockSpec((1, wide), lambda g, d, c: (0, g % groups))]
    state = [pltpu.VMEM((dk, dv), F32), pltpu.VMEM((1, dk), F32), pltpu.VMEM((1, 1), F32)]
    return pl.pallas_call(
        functools.partial(_mlstm_scan_body, hps=hps, nc=nc),
        grid=(batch * groups, 2, nc),
        in_specs=in_specs,
        out_specs=pl.BlockSpec((ln, wide), lambda g, d, c: (out_row_block(g, d, c), g % groups)),
        out_shape=jax.ShapeDtypeStruct((t, heads * dv), BF16),
        scratch_shapes=[pltpu.VMEM((s_len, wide), F32)] + state * hps,
        compiler_params=_params("parallel", "arbitrary", "arbitrary"),
        name=name,
    )(qk, qk, p, *([gates, gates] * hps), p, head_norm.reshape(1, heads * dv).astype(F32))


def _swap_halves(w, axis):
    half = w.shape[axis] // 2
    lo, hi = jnp.split(w, [half], axis=axis)
    return jnp.concatenate([hi, lo], axis=axis)


def _prep_mla_weights(w_in_t, w_uq, q_rank, kv_rank, heads):
    a1 = q_rank + kv_rank
    a2 = a1 + MLA_ROPE_DIM
    w_kr = w_in_t[a1:a2]
    w_in_p = jnp.concatenate([w_in_t[:a1], w_in_t[a2:], w_kr, _swap_halves(w_kr, 0)], axis=0).astype(BF16)
    wq = w_uq.reshape(q_rank, heads, MLA_NOPE_DIM + MLA_ROPE_DIM)
    rope_cols = wq[..., MLA_NOPE_DIM:]
    w_uq_p = jnp.concatenate([wq[..., :MLA_NOPE_DIM], rope_cols, _swap_halves(rope_cols, 2)], axis=-1)
    return w_in_p, w_uq_p.reshape(q_rank, heads * MLA_HEAD_COLS).astype(BF16)


def _rope_tables(positions):
    inv_freq = 1.0 / (ROPE_THETA ** (jnp.arange(0, MLA_ROPE_DIM, 2, dtype=F32) / MLA_ROPE_DIM))
    ang = positions.astype(F32).reshape(-1, 1) * inv_freq
    cos, sin = jnp.cos(ang), jnp.sin(ang)
    k_tab = jnp.concatenate([cos, cos, -sin, sin], axis=1)
    scale = (MLA_NOPE_DIM + MLA_ROPE_DIM) ** -0.5 * math.log2(math.e)
    q_tab = jnp.concatenate([jnp.ones((ang.shape[0], MLA_NOPE_DIM), F32), k_tab], axis=1) * scale
    return q_tab, k_tab


def kernel(x, positions, mem, attn_norm, ffn_norm, mem_norm, final_norm, mla_w_in, mla_q_norm, mla_kv_norm,
           mla_w_uq, mla_w_ukv, mlstm_w_in, mlstm_conv_w, mlstm_gate_b, mlstm_head_norm, w_mem_kv, w_out,
           w_gu, w_down):
    batch, s_len, d_model = x.shape
    depth = attn_norm.shape[0]
    n_mem = mem.shape[1]
    t = batch * s_len
    mem_width = d_model // 4
    main_width = d_model - mem_width
    mla_heads = main_width // MLA_V_DIM
    mlstm_heads = main_width // MLSTM_V_DIM
    qk_width = mlstm_heads * MLSTM_QK_DIM
    v_width = mlstm_heads * MLSTM_V_DIM
    q_rank = mla_q_norm.shape[1]
    kv_rank = mla_kv_norm.shape[1]
    d_ff = w_down.shape[1]

    tm = _pick(t, (1024, 512, 256, 128))
    q_tab, k_tab = _rope_tables(positions)
    mem_n = _rmsnorm(mem.reshape(batch * n_mem, d_model), mem_norm, width=d_model, name="mem_norm")
    mla_w_in_t = jnp.swapaxes(mla_w_in, 1, 2)
    mlstm_w_in_t = jnp.swapaxes(mlstm_w_in, 1, 2)
    h = x.reshape(t, d_model)

    for i in range(depth):
        j = i // N_MIXERS
        hn = _rmsnorm(h, attn_norm[i], width=d_model, name="attn_norm")
        mkv = _matmul(mem_n, w_mem_kv, layer=i, out_dtype=BF16, tm=_pick(batch * n_mem, (512, 256, 128)),
                      tn=_pick(2 * mem_width, (512, 256)), name="mem_kv")
        if i % N_MIXERS == 0:
            w_in_p, w_uq_p = _prep_mla_weights(mla_w_in_t[j], mla_w_uq[j], q_rank, kv_rank, mla_heads)
            n_in = w_in_p.shape[0]
            p = _matmul(hn, w_in_p, out_dtype=F32, tm=tm, tn=_pick(n_in, (896, 768, 640, 512, 384, 256, 128)),
                        name="mla_in_proj", transposed=True, ring=True)
            cq = _rmsnorm(p, mla_q_norm[j], width=q_rank, col_block=0, name="mla_q_norm")
            ckv = _rmsnorm(p, mla_kv_norm[j], width=kv_rank, col_block=q_rank // kv_rank, name="mla_kv_norm")
            head_cols = mla_heads * MLA_HEAD_COLS
            tm_small_k = _pick(t, (2048, 1024, 512, 256, 128))
            q = _matmul_mul(cq, w_uq_p, q_tab, out_dtype=BF16, tm=tm_small_k,
                            tn=_pick(head_cols, (1024, 768, 512, 256)), name="mla_q_proj")
            kv = _matmul(ckv, mla_w_ukv, layer=j, out_dtype=BF16, tm=tm_small_k,
                         tn=_pick(head_cols, (1536, 1024, 768, 512, 256)), name="mla_kv_proj")
            q_mem_src, q_mem_col = p, q_rank + kv_rank
            kr = _krope(p, k_tab, col_block=(q_mem_col + mem_width) // (2 * MLA_ROPE_DIM))
            main = _flash_attention(q, kv, kr, batch=batch, heads=mla_heads)
        else:
            g_col = 2 * qk_width + 2 * v_width
            w_gates_t = mlstm_w_in_t[j, g_col:g_col + 4 * mlstm_heads]
            w_q_mem_t = mlstm_w_in_t[j, g_col + 4 * mlstm_heads:]
            p = _matmul(hn, mlstm_w_in_t, layer=j, n=g_col, out_dtype=BF16, tm=tm, tn=_pick(g_col, (512, 256, 128)),
                        name="mlstm_in_proj", transposed=True, ring=True)
            q_mem_src = _matmul(hn, w_q_mem_t, out_dtype=BF16, tm=tm, tn=_pick(mem_width, (512, 256, 128)),
                                name="mlstm_qmem_proj", transposed=True)
            q_mem_col = 0
            gates = _mlstm_gates(w_gates_t, hn, mlstm_gate_b[j], heads=mlstm_heads)
            gates = gates.reshape(-1, 1, min(MLSTM_CHUNK, s_len))
            qk = _mlstm_conv(p, mlstm_conv_w[j], batch=batch, width=2 * qk_width, q_width=qk_width)
            main = _mlstm_scan(qk, p, gates, mlstm_head_norm[j], batch=batch, heads=mlstm_heads,
                               v_col=2 * qk_width, o_col=2 * qk_width + v_width)
        mem_out = _memory_attention(q_mem_src, mkv, batch=batch, q_col=q_mem_col, width=mem_width)
        h = _matmul_res2(main, mem_out, w_out, h, layer=i, tm=tm, tn=_pick(d_model, (512, 256)), name="out_proj",
                         ring=True)
        hn = _rmsnorm(h, ffn_norm[i], width=d_model, name="ffn_norm")
        act, w_down_bf16 = _matmul_swiglu(hn, w_gu, layer=i, tm=tm, tn=_pick(d_ff, (256, 128)), name="ffn_up",
                                          ring=True, side=(w_down, i))
        h = _matmul_res(act, w_down_bf16[None], h, layer=0, tm=_pick(t, (512, 256, 128)),
                        tn=_pick(d_model, (512, 256)), name="ffn_down")
    out = _rmsnorm(h, final_norm, width=d_model, out_dtype=x.dtype, name="final_norm")
    return out.reshape(batch, s_len, d_model)
```

```python
import functools
import math

import jax
import jax.numpy as jnp
from jax import lax
from jax.experimental import pallas as pl
from jax.experimental.pallas import tpu as pltpu

F32 = jnp.float32
BF16 = jnp.bfloat16

EPS = 1e-6
ROPE_THETA = 10000.0
MLA_NOPE_DIM = 128
MLA_ROPE_DIM = 64
MLA_V_DIM = 128
MLA_HEAD_COLS = 2 * MLA_NOPE_DIM
MLSTM_V_DIM = 512
MLSTM_QK_DIM = MLSTM_V_DIM // 2
MLSTM_CHUNK = 512
MEM_HEADS = 4
IGATE_CAP = 15.0
STAB_INIT = -1e30
N_MIXERS = 2

V7X_VMEM_BYTES = 64 * 1024 * 1024
VMEM_LIMIT_BYTES = V7X_VMEM_BYTES - 8 * 1024 * 1024


def _params(*semantics):
    return pltpu.CompilerParams(dimension_semantics=semantics, vmem_limit_bytes=VMEM_LIMIT_BYTES)


def _pick(n, candidates):
    for c in candidates:
        if n % c == 0:
            return c
    return n


def _dot(a, b):
    return jnp.dot(a, b, preferred_element_type=F32)


def _dot_nt(a, b):
    return lax.dot_general(a, b, (((1,), (1,)), ((), ())), preferred_element_type=F32)


def _dot_tn(a, b):
    return lax.dot_general(a, b, (((0,), (0,)), ((), ())), preferred_element_type=F32)


def _rmsnorm_body(x_ref, g_ref, o_ref):
    x = x_ref[...].astype(F32)
    ms = jnp.mean(x * x, axis=-1, keepdims=True)
    o_ref[...] = (x * lax.rsqrt(ms + EPS) * g_ref[...]).astype(o_ref.dtype)


def _rmsnorm(x, g, *, width, col_block=0, out_dtype=BF16, name="rmsnorm"):
    t = x.shape[0]
    tm = _pick(t, (512, 256, 128, 64, 8))
    return pl.pallas_call(
        _rmsnorm_body,
        grid=(t // tm,),
        in_specs=[pl.BlockSpec((tm, width), lambda i: (i, col_block)),
                  pl.BlockSpec((1, width), lambda i: (0, 0))],
        out_specs=pl.BlockSpec((tm, width), lambda i: (i, 0)),
        out_shape=jax.ShapeDtypeStruct((t, width), out_dtype),
        compiler_params=_params("parallel"),
        name=name,
    )(x, g.reshape(1, width).astype(F32))


def _ep_plain(mm, a, w, extra, o_ref):
    o_ref[...] = mm(a[0].astype(BF16), w[0][...]).astype(o_ref.dtype)


def _ep_mul(mm, a, w, extra, o_ref):
    table = extra[0][...]
    reps = o_ref.shape[1] // table.shape[1]
    if reps > 1:
        table = jnp.concatenate([table] * reps, axis=1)
    o_ref[...] = (mm(a[0].astype(BF16), w[0][...]) * table).astype(o_ref.dtype)


def _ep_residual(mm, a, w, extra, o_ref):
    acc = mm(a[0], w[0][...])
    for a_val, w_ref in zip(a[1:], w[1:]):
        acc = acc + mm(a_val, w_ref[...])
    o_ref[...] = extra[0][...] + acc


def _ep_swiglu(mm, a, w, extra, o_ref):
    g = mm(a[0], w[0][...])
    u = mm(a[0], w[1][...])
    o_ref[...] = (g * jax.nn.sigmoid(g) * u).astype(o_ref.dtype)


RING_SLOTS = 3


def _ring_fetch(a_hbm, bufs, sems, *, tm, n_m, total):
    step = pl.program_id(0) * n_m + pl.program_id(1)

    def copy(op, st):
        row = (st % n_m) * tm
        if not isinstance(row, int):
            row = pl.multiple_of(row, tm)
        slot = st % RING_SLOTS
        return pltpu.make_async_copy(a_hbm[op].at[pl.ds(row, tm), :], bufs[op].at[slot], sems.at[op, slot])

    @pl.when(step == 0)
    def _():
        for op in range(len(a_hbm)):
            for st in range(min(RING_SLOTS - 1, total)):
                copy(op, st).start()

    @pl.when(step + (RING_SLOTS - 1) < total)
    def _():
        for op in range(len(a_hbm)):
            copy(op, step + (RING_SLOTS - 1)).start()

    for op in range(len(a_hbm)):
        copy(op, step).wait()
    slot = step % RING_SLOTS
    return [buf[slot] for buf in bufs]


def _mm_body(*refs, n_a, n_w, n_extra, epilogue, cast, transposed, ring, side):
    a = refs[:n_a]
    w = refs[n_a:n_a + n_w]
    extra = refs[n_a + n_w:n_a + n_w + n_extra]
    n_in = n_a + n_w + n_extra
    o_ref = refs[n_in]
    scratch = refs[n_in + 1:]
    if side:
        refs[n_in + 2][...] = refs[n_in][...].astype(BF16)
        o_ref = refs[n_in + 1]
        scratch = refs[n_in + 3:]
    if cast:
        w_scratch, scratch = scratch[:n_w], scratch[n_w:]

        @pl.when(pl.program_id(1) == 0)
        def _():
            for w_ref, s_ref in zip(w, w_scratch):
                s_ref[...] = w_ref[...].astype(BF16)

        w = w_scratch
    if ring is None:
        a_vals = [a_ref[...] for a_ref in a]
    else:
        a_vals = _ring_fetch(a, scratch[:n_a], scratch[n_a], **ring)
    epilogue(_dot_nt if transposed else _dot, a_vals, w, extra, o_ref)


def _mm(name, epilogue, a_ops, w_ops, extra_ops, *, m, n, tm, tn, out_dtype, transposed=False, ring=False,
        side=None):
    e_specs = [pl.BlockSpec(bs, im) for _, bs, im in extra_ops]
    n_m = m // tm
    out_specs = pl.BlockSpec((tm, tn), lambda j, i: (i, j))
    out_shape = jax.ShapeDtypeStruct((m, n), out_dtype)
    side_ops = []
    if side is not None:
        side_arr, side_layer = side
        _, side_rows, side_cols = side_arr.shape
        slab = side_rows // ((n // tn) * n_m)
        assert slab * (n // tn) * n_m == side_rows and slab % 16 == 0
        e_specs.append(pl.BlockSpec((None, slab, side_cols), lambda j, i: (side_layer, j * n_m + i, 0)))
        out_specs = [out_specs, pl.BlockSpec((slab, side_cols), lambda j, i: (j * n_m + i, 0))]
        out_shape = [out_shape, jax.ShapeDtypeStruct((side_rows, side_cols), BF16)]
        side_ops = [side_arr]
    ring_args = None
    ring_scratch = []
    if ring:
        a_specs = [pl.BlockSpec(memory_space=pl.ANY) for _ in a_ops]
        ring_args = dict(tm=tm, n_m=n_m, total=(n // tn) * n_m)
        ring_scratch = [pltpu.VMEM((RING_SLOTS,) + tuple(bs), arr.dtype) for arr, bs, _ in a_ops]
        ring_scratch.append(pltpu.SemaphoreType.DMA((len(a_ops), RING_SLOTS)))
    else:
        a_specs = [pl.BlockSpec(bs, im) for _, bs, im in a_ops]

    def w_spec(layer, rows, rb, co):
        if transposed:
            return pl.BlockSpec((None, tn, rows), lambda j, i: (layer, j + co, rb))
        return pl.BlockSpec((None, rows, tn), lambda j, i: (layer, rb, j + co))

    w_specs = [w_spec(layer, rows, rb, co) for _, layer, rows, rb, co in w_ops]
    cast = w_ops[0][0].dtype != BF16
    scratch = []
    if cast:
        scratch = [pltpu.VMEM((tn, rows) if transposed else (rows, tn), BF16) for _, _, rows, _, _ in w_ops]
    return pl.pallas_call(
        functools.partial(_mm_body, n_a=len(a_ops), n_w=len(w_ops), n_extra=len(extra_ops),
                          epilogue=epilogue, cast=cast, transposed=transposed, ring=ring_args,
                          side=side is not None),
        grid=(n // tn, n_m),
        in_specs=a_specs + w_specs + e_specs,
        out_specs=out_specs,
        out_shape=out_shape,
        scratch_shapes=scratch + ring_scratch,
        compiler_params=_params("arbitrary", "arbitrary"),
        name=name,
    )(*[op[0] for op in a_ops], *[op[0] for op in w_ops], *[op[0] for op in extra_ops], *side_ops)


def _rows(tm, k, col_block=0):
    return (tm, k), lambda j, i: (i, col_block)


def _as3d(w):
    return w if w.ndim == 3 else w[None]


def _matmul(a, w, *, layer=0, n=None, out_dtype, tm, tn, name, transposed=False, ring=False):
    w = _as3d(w)
    m, k = a.shape
    if n is None:
        n = w.shape[1] if transposed else w.shape[2]
    return _mm(name, _ep_plain, [(a, *_rows(tm, k))], [(w, layer, k, 0, 0)], [],
               m=m, n=n, tm=tm, tn=tn, out_dtype=out_dtype, transposed=transposed, ring=ring)


def _matmul_mul(a, w, table, *, out_dtype, tm, tn, name):
    w = _as3d(w)
    m, k = a.shape
    c = table.shape[1]
    return _mm(name, _ep_mul, [(a, *_rows(tm, k))], [(w, 0, k, 0, 0)], [(table, (tm, c), lambda j, i: (i, 0))],
               m=m, n=w.shape[2], tm=tm, tn=tn, out_dtype=out_dtype)


def _matmul_res(a, w, res, *, layer, tm, tn, name):
    m, k = a.shape
    return _mm(name, _ep_residual, [(a, *_rows(tm, k))], [(w, layer, k, 0, 0)],
               [(res, (tm, tn), lambda j, i: (i, j))], m=m, n=w.shape[2], tm=tm, tn=tn, out_dtype=F32)


def _matmul_res2(a1, a2, w, res, *, layer, tm, tn, name, ring=False):
    m, k1 = a1.shape
    k2 = a2.shape[1]
    assert k1 % k2 == 0
    return _mm(name, _ep_residual, [(a1, *_rows(tm, k1)), (a2, *_rows(tm, k2))],
               [(w, layer, k1, 0, 0), (w, layer, k2, k1 // k2, 0)],
               [(res, (tm, tn), lambda j, i: (i, j))], m=m, n=w.shape[2], tm=tm, tn=tn, out_dtype=F32, ring=ring)


def _matmul_swiglu(a, w_gu, *, layer, tm, tn, name, ring=False, side=None):
    m, k = a.shape
    f = w_gu.shape[2] // 2
    return _mm(name, _ep_swiglu, [(a, *_rows(tm, k))], [(w_gu, layer, k, 0, 0), (w_gu, layer, k, 0, f // tn)], [],
               m=m, n=f, tm=tm, tn=tn, out_dtype=BF16, ring=ring, side=side)


def _krope_body(x_ref, t_ref, o_ref):
    y = x_ref[...] * t_ref[...]
    o_ref[...] = (y + pltpu.roll(y, MLA_ROPE_DIM, 1)).astype(o_ref.dtype)


def _krope(p, table, *, col_block, name="mla_krope"):
    t = p.shape[0]
    tm = _pick(t, (1024, 512, 256, 128, 8))
    w = 2 * MLA_ROPE_DIM
    return pl.pallas_call(
        _krope_body,
        grid=(t // tm,),
        in_specs=[pl.BlockSpec((tm, w), lambda i: (i, col_block)),
                  pl.BlockSpec((tm, w), lambda i: (i, 0))],
        out_specs=pl.BlockSpec((tm, w), lambda i: (i, 0)),
        out_shape=jax.ShapeDtypeStruct((t, w), BF16),
        compiler_params=_params("parallel"),
        name=name,
    )(p, table)


def _flash_body(q_ref, kn_ref, v_ref, kr_ref, o_ref, k_scr, *, tk):
    @pl.when(pl.program_id(2) == 0)
    def _():
        k_scr[:, :MLA_NOPE_DIM] = kn_ref[...]
        k_scr[:, MLA_NOPE_DIM:] = kr_ref[...]

    q = q_ref[...]
    tq = q.shape[0]
    s_len = k_scr.shape[0]
    m = jnp.full((tq, 1), -jnp.inf, F32)
    l = jnp.zeros((tq, 1), F32)
    acc = jnp.zeros((tq, MLA_V_DIM), F32)
    for c in range(s_len // tk):
        s = _dot_nt(q, k_scr[c * tk:(c + 1) * tk, :])
        m_new = jnp.maximum(m, jnp.max(s, axis=-1, keepdims=True))
        alpha = jnp.exp2(m - m_new)
        p = jnp.exp2(s - m_new)
        l = alpha * l + jnp.sum(p, axis=-1, keepdims=True)
        acc = alpha * acc + _dot(p.astype(BF16), v_ref[c * tk:(c + 1) * tk, :])
        m = m_new
    o_ref[...] = (acc / l).astype(o_ref.dtype)


def _flash_attention(q, kv, kr, *, batch, heads, name="mla_flash"):
    t = q.shape[0]
    s_len = t // batch
    tq = _pick(s_len, (2048, 1024, 512, 256, 128))
    tk = _pick(s_len, (1024, 512, 256, 128))
    nq = s_len // tq
    return pl.pallas_call(
        functools.partial(_flash_body, tk=tk),
        grid=(batch, heads, nq),
        in_specs=[pl.BlockSpec((tq, MLA_HEAD_COLS), lambda b, h, i: (b * nq + i, h)),
                  pl.BlockSpec((s_len, MLA_NOPE_DIM), lambda b, h, i: (b, 2 * h)),
                  pl.BlockSpec((s_len, MLA_V_DIM), lambda b, h, i: (b, 2 * h + 1)),
                  pl.BlockSpec((s_len, 2 * MLA_ROPE_DIM), lambda b, h, i: (b, 0))],
        out_specs=pl.BlockSpec((tq, MLA_V_DIM), lambda b, h, i: (b * nq + i, h)),
        out_shape=jax.ShapeDtypeStruct((t, heads * MLA_V_DIM), BF16),
        scratch_shapes=[pltpu.VMEM((s_len, MLA_HEAD_COLS), BF16)],
        compiler_params=_params("parallel", "parallel", "arbitrary"),
        name=name,
    )(q, kv, kv, kr)


def _memattn_body(*refs, scale):
    q_refs = refs[:MEM_HEADS]
    mk_ref, mv_ref, o_ref = refs[MEM_HEADS:]
    dh = q_refs[0].shape[1]
    for h in range(MEM_HEADS):
        q = (q_refs[h][...].astype(F32) * scale).astype(BF16)
        s = _dot_nt(q, mk_ref[:, h * dh:(h + 1) * dh])
        p = jnp.exp(s - jnp.max(s, axis=-1, keepdims=True))
        l = jnp.sum(p, axis=-1, keepdims=True)
        o = _dot((p / l).astype(BF16), mv_ref[:, h * dh:(h + 1) * dh])
        o_ref[:, h * dh:(h + 1) * dh] = o.astype(o_ref.dtype)


def _memory_attention(p, mkv, *, batch, q_col, width, name="mem_attn"):
    t = p.shape[0]
    s_len = t // batch
    n_mem = mkv.shape[0] // batch
    dh = width // MEM_HEADS
    tq = _pick(s_len, (512, 256, 128))
    nq = s_len // tq
    q_specs = [pl.BlockSpec((tq, dh), functools.partial(lambda b, i, h: (b * nq + i, q_col // dh + h), h=h))
               for h in range(MEM_HEADS)]
    return pl.pallas_call(
        functools.partial(_memattn_body, scale=dh ** -0.5),
        grid=(batch, nq),
        in_specs=q_specs + [pl.BlockSpec((n_mem, width), lambda b, i: (b, 0)),
                            pl.BlockSpec((n_mem, width), lambda b, i: (b, 1))],
        out_specs=pl.BlockSpec((tq, width), lambda b, i: (b * nq + i, 0)),
        out_shape=jax.ShapeDtypeStruct((t, width), BF16),
        compiler_params=_params("parallel", "parallel"),
        name=name,
    )(*([p] * MEM_HEADS), mkv, mkv)


def _gates_body(w_ref, x_ref, b_ref, f_ref, bwd_ref, o_ref, cs_ref, *, chunk):
    g = _dot_nt(w_ref[...].astype(BF16), x_ref[...]) + b_ref[...]
    igate = IGATE_CAP * jnp.tanh(g / IGATE_CAP)
    log_fgate = jnp.minimum(g, 0.0) - jnp.log1p(jnp.exp(-jnp.abs(g)))
    act = jnp.where(f_ref[...] > 0.5, log_fgate, igate) * math.log2(math.e)
    o_ref[...] = act
    tn = act.shape[1]
    pos = lax.broadcasted_iota(jnp.int32, act.shape, 1) & (chunk - 1)
    fwd = act
    bwd = act
    shift = 1
    while shift < chunk:
        fwd = fwd + jnp.where(pos >= shift, pltpu.roll(fwd, shift, 1), 0.0)
        bwd = bwd + jnp.where(pos < chunk - shift, pltpu.roll(bwd, tn - shift, 1), 0.0)
        shift *= 2
    cs_ref[...] = jnp.where(bwd_ref[...] > 0.5, bwd, fwd)


def _mlstm_gates(w_t, hn, bias, *, heads, chunk, name="mlstm_gates"):
    rows, k = w_t.shape
    t = hn.shape[0]
    tn = _pick(t, (1024, 512, 256, 128))
    assert tn % chunk == 0 and chunk & (chunk - 1) == 0
    kind = jnp.arange(rows) // heads
    column = lambda flag: flag.astype(F32).reshape(rows, 1)
    small = pl.BlockSpec((rows, 1), lambda i: (0, 0))
    out_sds = jax.ShapeDtypeStruct((rows, t), F32)
    return pl.pallas_call(
        functools.partial(_gates_body, chunk=chunk),
        grid=(t // tn,),
        in_specs=[pl.BlockSpec((rows, k), lambda i: (0, 0)), pl.BlockSpec((tn, k), lambda i: (i, 0)), small, small, small],
        out_specs=[pl.BlockSpec((rows, tn), lambda i: (0, i))] * 2,
        out_shape=[out_sds, out_sds],
        compiler_params=_params("parallel"),
        name=name,
    )(w_t, hn, bias.reshape(rows, 1).astype(F32), column(kind % 2 == 1), column(kind >= 2))


def _conv_body(x_ref, prev_ref, next_ref, w_ref, o_ref, *, blocks_per_seq, q_blocks, q_scale):
    i = pl.program_id(0)
    j = pl.program_id(1)
    x = x_ref[...].astype(F32)
    ts = x.shape[0]
    pos = i % blocks_per_seq
    halo = prev_ref.shape[0]
    prev_row = jnp.where(pos == 0, 0.0, prev_ref[...].astype(F32)[halo - 1:halo, :])
    next_row = jnp.where(pos == blocks_per_seq - 1, 0.0, next_ref[...].astype(F32)[0:1, :])
    row = lax.broadcasted_iota(jnp.int32, x.shape, 0)
    x_prev = jnp.where(row == 0, prev_row, pltpu.roll(x, 1, 0))
    x_next = jnp.where(row == ts - 1, next_row, pltpu.roll(x, ts - 1, 0))
    y = x_prev * w_ref[0:1, :] + x * w_ref[1:2, :] + x_next * w_ref[2:3, :]
    y = y * jax.nn.sigmoid(y)
    scale = jnp.where(j < q_blocks, q_scale, 1.0)
    o_ref[...] = (y * scale).astype(o_ref.dtype)


def _mlstm_conv(p, conv_w, *, batch, width, q_width, name="mlstm_conv"):
    t = p.shape[0]
    s_len = t // batch
    ts = _pick(s_len, (512, 256, 128))
    tc = _pick(q_width, (1536, 1024, 512, 256, 128))
    halo = 8 * (4 // p.dtype.itemsize)
    hb = ts // halo
    return pl.pallas_call(
        functools.partial(_conv_body, blocks_per_seq=s_len // ts, q_blocks=q_width // tc,
                          q_scale=MLSTM_QK_DIM ** -0.5),
        grid=(t // ts, width // tc),
        in_specs=[pl.BlockSpec((ts, tc), lambda i, j: (i, j)),
                  pl.BlockSpec((halo, tc), lambda i, j: (jnp.maximum(i * hb - 1, 0), j)),
                  pl.BlockSpec((halo, tc), lambda i, j: (jnp.minimum((i + 1) * hb, t // halo - 1), j)),
                  pl.BlockSpec((3, tc), lambda i, j: (0, j))],
        out_specs=pl.BlockSpec((ts, tc), lambda i, j: (i, j)),
        out_shape=jax.ShapeDtypeStruct((t, width), BF16),
        compiler_params=_params("parallel", "parallel"),
        name=name,
    )(p, p, p, conv_w.astype(F32))


def _mlstm_chunk(q, k, v, ig, lf, cs_row, ig_col, cs_col, c_scr, n_scr, m_scr, *, sign):
    ln = q.shape[0]
    t_idx = lax.broadcasted_iota(jnp.int32, (ln, ln), 0)
    j_idx = lax.broadcasted_iota(jnp.int32, (ln, ln), 1)
    mask = (j_idx - t_idx) * sign <= 0
    m_prev = m_scr[...]

    dmat = jnp.where(mask, cs_col - cs_row + ig, -jnp.inf)
    inter = cs_col + m_prev
    m_row = jnp.maximum(jnp.max(dmat, axis=1, keepdims=True), inter)
    s = _dot_nt(q, k) * jnp.exp2(dmat - m_row)
    w_inter = jnp.exp2(inter - m_row)
    num = _dot(s.astype(BF16), v) + w_inter * _dot(q, c_scr[...].astype(BF16))
    qn = jnp.sum(q.astype(F32) * n_scr[...], axis=1, keepdims=True)
    den = jnp.sum(s, axis=1, keepdims=True) + w_inter * qn
    h = num / jnp.maximum(jnp.abs(den), jnp.exp2(-m_row))

    b_end = jnp.sum(lf, axis=1, keepdims=True)
    w_tok_row = b_end - cs_row + ig
    w_tok_col = b_end - cs_col + ig_col
    m_new = jnp.maximum(b_end + m_prev, jnp.max(w_tok_row, axis=1, keepdims=True))
    decay = jnp.exp2(b_end + m_prev - m_new)
    wk = k.astype(F32) * jnp.exp2(w_tok_col - m_new)
    c_scr[...] = decay * c_scr[...] + _dot_tn(wk.astype(BF16), v)
    n_scr[...] = decay * n_scr[...] + jnp.sum(wk, axis=0, keepdims=True)
    m_scr[...] = m_new
    return h


def _mlstm_scan_body(*refs, hps, nc):
    q_ref, k_ref, v_ref = refs[:3]
    n_rows = 3 * hps
    row_refs = refs[3:3 + n_rows]
    col_ref, o_pre_ref, g_ref, out_ref = refs[3 + n_rows:7 + n_rows]
    hf_scr = refs[7 + n_rows]
    state = refs[8 + n_rows:]
    dk, dv = MLSTM_QK_DIM, MLSTM_V_DIM
    ln = q_ref.shape[0]
    d = pl.program_id(1)
    c = pl.program_id(2)

    @pl.when(c == 0)
    def _():
        for idx in range(0, len(state), 3):
            c_scr, n_scr, m_scr = state[idx:idx + 3]
            c_scr[...] = jnp.zeros_like(c_scr)
            n_scr[...] = jnp.zeros_like(n_scr)
            m_scr[...] = jnp.full_like(m_scr, STAB_INIT)

    hs = [_mlstm_chunk(q_ref[:, hh * dk:(hh + 1) * dk], k_ref[:, hh * dk:(hh + 1) * dk],
                       v_ref[:, hh * dv:(hh + 1) * dv], *[r[0] for r in row_refs[3 * hh:3 * hh + 3]],
                       col_ref[:, hh:hh + 1], col_ref[:, hps + hh:hps + hh + 1],
                       *state[3 * hh:3 * hh + 3], sign=1 - 2 * d) for hh in range(hps)]

    @pl.when(d == 0)
    def _():
        rows = pl.ds(pl.multiple_of(c * ln, ln), ln)
        for hh in range(hps):
            hf_scr[rows, hh * dv:(hh + 1) * dv] = hs[hh]

    @pl.when(d == 1)
    def _():
        rows = pl.ds(pl.multiple_of((nc - 1 - c) * ln, ln), ln)
        for hh in range(hps):
            cols = slice(hh * dv, (hh + 1) * dv)
            h = hf_scr[rows, cols] + hs[hh]
            ms = jnp.mean(h * h, axis=-1, keepdims=True)
            y = h * lax.rsqrt(ms + EPS) * g_ref[:, cols]
            out_ref[:, cols] = (y * jax.nn.sigmoid(o_pre_ref[:, cols].astype(F32))).astype(out_ref.dtype)


def _mlstm_scan(qk, p, gates, sums, head_norm, *, batch, heads, chunk_len, v_col, o_col, name="mlstm_scan"):
    t = qk.shape[0]
    s_len = t // batch
    ln = chunk_len
    nc = s_len // ln
    dk, dv = MLSTM_QK_DIM, MLSTM_V_DIM
    hps = 2 if heads % 2 == 0 else 1
    groups = heads // hps
    by_kind = lambda a: a.reshape(2, 2, groups, hps, t)
    cols = jnp.swapaxes(jnp.concatenate([by_kind(gates)[:, 0], by_kind(sums)[:, 1]], axis=2), 2, 3)
    gates = gates.reshape(-1, 1, ln)
    sums = sums.reshape(-1, 1, ln)

    def chunk(d, c):
        return c + d * (nc - 1 - 2 * c)

    def row_block(g, d, c):
        return (g // groups) * nc + chunk(d, c)

    def out_row_block(g, d, c):
        return (g // groups) * nc + nc - 1 - d * c

    def gate_block(kind, hh):
        def index(g, d, c):
            head = (g % groups) * hps + hh
            return (((2 * d + kind) * heads + head) * batch + g // groups) * nc + chunk(d, c), 0, 0
        return index

    wide = hps * dv
    in_specs = [pl.BlockSpec((ln, hps * dk), lambda g, d, c: (row_block(g, d, c), g % groups)),
                pl.BlockSpec((ln, hps * dk), lambda g, d, c: (row_block(g, d, c), groups + g % groups)),
                pl.BlockSpec((ln, wide), lambda g, d, c: (row_block(g, d, c), v_col // wide + g % groups))]
    row_ops = []
    for hh in range(hps):
        in_specs += [pl.BlockSpec((1, 1, ln), gate_block(0, hh)), pl.BlockSpec((1, 1, ln), gate_block(1, hh)),
                     pl.BlockSpec((1, 1, ln), gate_block(1, hh))]
        row_ops += [gates, gates, sums]
    in_specs += [pl.BlockSpec((None, None, ln, 2 * hps), lambda g, d, c: (d, g % groups, row_block(g, d, c), 0)),
                 pl.BlockSpec((ln, wide), lambda g, d, c: (out_row_block(g, d, c), o_col // wide + g % groups)),
                 pl.BlockSpec((1, wide), lambda g, d, c: (0, g % groups))]
    state = [pltpu.VMEM((dk, dv), F32), pltpu.VMEM((1, dk), F32), pltpu.VMEM((1, 1), F32)]
    return pl.pallas_call(
        functools.partial(_mlstm_scan_body, hps=hps, nc=nc),
        grid=(batch * groups, 2, nc),
        in_specs=in_specs,
        out_specs=pl.BlockSpec((ln, wide), lambda g, d, c: (out_row_block(g, d, c), g % groups)),
        out_shape=jax.ShapeDtypeStruct((t, heads * dv), BF16),
        scratch_shapes=[pltpu.VMEM((s_len, wide), F32)] + state * hps,
        compiler_params=_params("parallel", "arbitrary", "arbitrary"),
        name=name,
    )(qk, qk, p, *row_ops, cols, p, head_norm.reshape(1, heads * dv).astype(F32))


def _swap_halves(w, axis):
    half = w.shape[axis] // 2
    lo, hi = jnp.split(w, [half], axis=axis)
    return jnp.concatenate([hi, lo], axis=axis)


def _prep_mla_weights(w_in_t, w_uq, q_rank, kv_rank, heads):
    a1 = q_rank + kv_rank
    a2 = a1 + MLA_ROPE_DIM
    w_kr = w_in_t[a1:a2]
    w_in_p = jnp.concatenate([w_in_t[:a1], w_in_t[a2:], w_kr, _swap_halves(w_kr, 0)], axis=0).astype(BF16)
    wq = w_uq.reshape(q_rank, heads, MLA_NOPE_DIM + MLA_ROPE_DIM)
    rope_cols = wq[..., MLA_NOPE_DIM:]
    w_uq_p = jnp.concatenate([wq[..., :MLA_NOPE_DIM], rope_cols, _swap_halves(rope_cols, 2)], axis=-1)
    return w_in_p, w_uq_p.reshape(q_rank, heads * MLA_HEAD_COLS).astype(BF16)


def _rope_tables(positions):
    inv_freq = 1.0 / (ROPE_THETA ** (jnp.arange(0, MLA_ROPE_DIM, 2, dtype=F32) / MLA_ROPE_DIM))
    ang = positions.astype(F32).reshape(-1, 1) * inv_freq
    cos, sin = jnp.cos(ang), jnp.sin(ang)
    k_tab = jnp.concatenate([cos, cos, -sin, sin], axis=1)
    scale = (MLA_NOPE_DIM + MLA_ROPE_DIM) ** -0.5 * math.log2(math.e)
    q_tab = jnp.concatenate([jnp.ones((ang.shape[0], MLA_NOPE_DIM), F32), k_tab], axis=1) * scale
    return q_tab, k_tab


def kernel(x, positions, mem, attn_norm, ffn_norm, mem_norm, final_norm, mla_w_in, mla_q_norm, mla_kv_norm,
           mla_w_uq, mla_w_ukv, mlstm_w_in, mlstm_conv_w, mlstm_gate_b, mlstm_head_norm, w_mem_kv, w_out,
           w_gu, w_down):
    batch, s_len, d_model = x.shape
    depth = attn_norm.shape[0]
    n_mem = mem.shape[1]
    t = batch * s_len
    mem_width = d_model // 4
    main_width = d_model - mem_width
    mla_heads = main_width // MLA_V_DIM
    mlstm_heads = main_width // MLSTM_V_DIM
    qk_width = mlstm_heads * MLSTM_QK_DIM
    v_width = mlstm_heads * MLSTM_V_DIM
    q_rank = mla_q_norm.shape[1]
    kv_rank = mla_kv_norm.shape[1]
    d_ff = w_down.shape[1]

    tm = _pick(t, (1024, 512, 256, 128))
    q_tab, k_tab = _rope_tables(positions)
    mem_n = _rmsnorm(mem.reshape(batch * n_mem, d_model), mem_norm, width=d_model, name="mem_norm")
    mla_w_in_t = jnp.swapaxes(mla_w_in, 1, 2)
    mlstm_w_in_t = jnp.swapaxes(mlstm_w_in, 1, 2)
    h = x.reshape(t, d_model)

    for i in range(depth):
        j = i // N_MIXERS
        hn = _rmsnorm(h, attn_norm[i], width=d_model, name="attn_norm")
        mkv = _matmul(mem_n, w_mem_kv, layer=i, out_dtype=BF16, tm=_pick(batch * n_mem, (512, 256, 128)),
                      tn=_pick(2 * mem_width, (512, 256)), name="mem_kv")
        if i % N_MIXERS == 0:
            w_in_p, w_uq_p = _prep_mla_weights(mla_w_in_t[j], mla_w_uq[j], q_rank, kv_rank, mla_heads)
            n_in = w_in_p.shape[0]
            p = _matmul(hn, w_in_p, out_dtype=F32, tm=tm, tn=_pick(n_in, (896, 768, 640, 512, 384, 256, 128)),
                        name="mla_in_proj", transposed=True, ring=True)
            cq = _rmsnorm(p, mla_q_norm[j], width=q_rank, col_block=0, name="mla_q_norm")
            ckv = _rmsnorm(p, mla_kv_norm[j], width=kv_rank, col_block=q_rank // kv_rank, name="mla_kv_norm")
            head_cols = mla_heads * MLA_HEAD_COLS
            tm_small_k = _pick(t, (2048, 1024, 512, 256, 128))
            q = _matmul_mul(cq, w_uq_p, q_tab, out_dtype=BF16, tm=tm_small_k,
                            tn=_pick(head_cols, (1024, 768, 512, 256)), name="mla_q_proj")
            kv = _matmul(ckv, mla_w_ukv, layer=j, out_dtype=BF16, tm=tm_small_k,
                         tn=_pick(head_cols, (1536, 1024, 768, 512, 256)), name="mla_kv_proj")
            q_mem_src, q_mem_col = p, q_rank + kv_rank
            kr = _krope(p, k_tab, col_block=(q_mem_col + mem_width) // (2 * MLA_ROPE_DIM))
            main = _flash_attention(q, kv, kr, batch=batch, heads=mla_heads)
        else:
            g_col = 2 * qk_width + 2 * v_width
            w_gates_t = mlstm_w_in_t[j, g_col:g_col + 4 * mlstm_heads]
            w_q_mem_t = mlstm_w_in_t[j, g_col + 4 * mlstm_heads:]
            p = _matmul(hn, mlstm_w_in_t, layer=j, n=g_col, out_dtype=BF16, tm=tm, tn=_pick(g_col, (512, 256, 128)),
                        name="mlstm_in_proj", transposed=True, ring=True)
            q_mem_src = _matmul(hn, w_q_mem_t, out_dtype=BF16, tm=tm, tn=_pick(mem_width, (512, 256, 128)),
                                name="mlstm_qmem_proj", transposed=True)
            q_mem_col = 0
            chunk_len = min(MLSTM_CHUNK, s_len)
            gates, gate_sums = _mlstm_gates(w_gates_t, hn, mlstm_gate_b[j], heads=mlstm_heads, chunk=chunk_len)
            qk = _mlstm_conv(p, mlstm_conv_w[j], batch=batch, width=2 * qk_width, q_width=qk_width)
            main = _mlstm_scan(qk, p, gates, gate_sums, mlstm_head_norm[j], batch=batch, heads=mlstm_heads,
                               chunk_len=chunk_len, v_col=2 * qk_width, o_col=2 * qk_width + v_width)
        mem_out = _memory_attention(q_mem_src, mkv, batch=batch, q_col=q_mem_col, width=mem_width)
        h = _matmul_res2(main, mem_out, w_out, h, layer=i, tm=tm, tn=_pick(d_model, (512, 256)), name="out_proj",
                         ring=True)
        hn = _rmsnorm(h, ffn_norm[i], width=d_model, name="ffn_norm")
        act, w_down_bf16 = _matmul_swiglu(hn, w_gu, layer=i, tm=tm, tn=_pick(d_ff, (256, 128)), name="ffn_up",
                                          ring=True, side=(w_down, i))
        h = _matmul_res(act, w_down_bf16[None], h, layer=0, tm=_pick(t, (512, 256, 128)),
                        tn=_pick(d_model, (512, 256)), name="ffn_down")
    out = _rmsnorm(h, final_norm, width=d_model, out_dtype=x.dtype, name="final_norm")
    return out.reshape(batch, s_len, d_model)
```

```python
import functools
import math

import jax
import jax.numpy as jnp
from jax import lax
from jax.experimental import pallas as pl
from jax.experimental.pallas import tpu as pltpu

F32 = jnp.float32
BF16 = jnp.bfloat16

EPS = 1e-6
ROPE_THETA = 10000.0
MLA_NOPE_DIM = 128
MLA_ROPE_DIM = 64
MLA_V_DIM = 128
MLA_HEAD_COLS = 2 * MLA_NOPE_DIM
MLSTM_V_DIM = 512
MLSTM_QK_DIM = MLSTM_V_DIM // 2
MLSTM_CHUNK = 512
MEM_HEADS = 4
IGATE_CAP = 15.0
STAB_INIT = -1e30
N_MIXERS = 2

V7X_VMEM_BYTES = 64 * 1024 * 1024
VMEM_LIMIT_BYTES = V7X_VMEM_BYTES - 8 * 1024 * 1024


def _params(*semantics):
    return pltpu.CompilerParams(dimension_semantics=semantics, vmem_limit_bytes=VMEM_LIMIT_BYTES)


def _pick(n, candidates):
    for c in candidates:
        if n % c == 0:
            return c
    return n


def _dot(a, b):
    return jnp.dot(a, b, preferred_element_type=F32)


def _dot_nt(a, b):
    return lax.dot_general(a, b, (((1,), (1,)), ((), ())), preferred_element_type=F32)


def _dot_tn(a, b):
    return lax.dot_general(a, b, (((0,), (0,)), ((), ())), preferred_element_type=F32)


def _rmsnorm_body(x_ref, g_ref, o_ref):
    x = x_ref[...].astype(F32)
    ms = jnp.mean(x * x, axis=-1, keepdims=True)
    o_ref[...] = (x * lax.rsqrt(ms + EPS) * g_ref[...]).astype(o_ref.dtype)


def _rmsnorm(x, g, *, width, col_block=0, out_dtype=BF16, name="rmsnorm"):
    t = x.shape[0]
    tm = _pick(t, (512, 256, 128, 64, 8))
    return pl.pallas_call(
        _rmsnorm_body,
        grid=(t // tm,),
        in_specs=[pl.BlockSpec((tm, width), lambda i: (i, col_block)),
                  pl.BlockSpec((1, width), lambda i: (0, 0))],
        out_specs=pl.BlockSpec((tm, width), lambda i: (i, 0)),
        out_shape=jax.ShapeDtypeStruct((t, width), out_dtype),
        compiler_params=_params("parallel"),
        name=name,
    )(x, g.reshape(1, width).astype(F32))


def _ep_plain(mm, a, w, extra, o_ref):
    o_ref[...] = mm(a[0].astype(BF16), w[0][...]).astype(o_ref.dtype)


def _ep_mul(mm, a, w, extra, o_ref):
    table = extra[0][...]
    reps = o_ref.shape[1] // table.shape[1]
    if reps > 1:
        table = jnp.concatenate([table] * reps, axis=1)
    o_ref[...] = (mm(a[0].astype(BF16), w[0][...]) * table).astype(o_ref.dtype)


def _ep_residual(mm, a, w, extra, o_ref):
    acc = mm(a[0], w[0][...])
    for a_val, w_ref in zip(a[1:], w[1:]):
        acc = acc + mm(a_val, w_ref[...])
    o_ref[...] = extra[0][...] + acc


def _ep_swiglu(mm, a, w, extra, o_ref):
    g = mm(a[0], w[0][...])
    u = mm(a[0], w[1][...])
    o_ref[...] = (g * jax.nn.sigmoid(g) * u).astype(o_ref.dtype)


RING_SLOTS = 3


def _ring_fetch(a_hbm, bufs, sems, *, tm, n_m, total):
    step = pl.program_id(0) * n_m + pl.program_id(1)

    def copy(op, st):
        row = (st % n_m) * tm
        if not isinstance(row, int):
            row = pl.multiple_of(row, tm)
        slot = st % RING_SLOTS
        return pltpu.make_async_copy(a_hbm[op].at[pl.ds(row, tm), :], bufs[op].at[slot], sems.at[op, slot])

    @pl.when(step == 0)
    def _():
        for op in range(len(a_hbm)):
            for st in range(min(RING_SLOTS - 1, total)):
                copy(op, st).start()

    @pl.when(step + (RING_SLOTS - 1) < total)
    def _():
        for op in range(len(a_hbm)):
            copy(op, step + (RING_SLOTS - 1)).start()

    for op in range(len(a_hbm)):
        copy(op, step).wait()
    slot = step % RING_SLOTS
    return [buf[slot] for buf in bufs]


def _mm_body(*refs, n_a, n_w, n_extra, epilogue, cast, transposed, ring, side):
    a = refs[:n_a]
    w = refs[n_a:n_a + n_w]
    extra = refs[n_a + n_w:n_a + n_w + n_extra]
    n_in = n_a + n_w + n_extra
    o_ref = refs[n_in]
    scratch = refs[n_in + 1:]
    if side:
        refs[n_in + 2][...] = refs[n_in][...].astype(BF16)
        o_ref = refs[n_in + 1]
        scratch = refs[n_in + 3:]
    if cast:
        w_scratch, scratch = scratch[:n_w], scratch[n_w:]

        @pl.when(pl.program_id(1) == 0)
        def _():
            for w_ref, s_ref in zip(w, w_scratch):
                s_ref[...] = w_ref[...].astype(BF16)

        w = w_scratch
    if ring is None:
        a_vals = [a_ref[...] for a_ref in a]
    else:
        a_vals = _ring_fetch(a, scratch[:n_a], scratch[n_a], **ring)
    epilogue(_dot_nt if transposed else _dot, a_vals, w, extra, o_ref)


def _mm(name, epilogue, a_ops, w_ops, extra_ops, *, m, n, tm, tn, out_dtype, transposed=False, ring=False,
        side=None):
    e_specs = [pl.BlockSpec(bs, im) for _, bs, im in extra_ops]
    n_m = m // tm
    out_specs = pl.BlockSpec((tm, tn), lambda j, i: (i, j))
    out_shape = jax.ShapeDtypeStruct((m, n), out_dtype)
    side_ops = []
    if side is not None:
        side_arr, side_layer = side
        _, side_rows, side_cols = side_arr.shape
        slab = side_rows // ((n // tn) * n_m)
        assert slab * (n // tn) * n_m == side_rows and slab % 16 == 0
        e_specs.append(pl.BlockSpec((None, slab, side_cols), lambda j, i: (side_layer, j * n_m + i, 0)))
        out_specs = [out_specs, pl.BlockSpec((slab, side_cols), lambda j, i: (j * n_m + i, 0))]
        out_shape = [out_shape, jax.ShapeDtypeStruct((side_rows, side_cols), BF16)]
        side_ops = [side_arr]
    ring_args = None
    ring_scratch = []
    if ring:
        a_specs = [pl.BlockSpec(memory_space=pl.ANY) for _ in a_ops]
        ring_args = dict(tm=tm, n_m=n_m, total=(n // tn) * n_m)
        ring_scratch = [pltpu.VMEM((RING_SLOTS,) + tuple(bs), arr.dtype) for arr, bs, _ in a_ops]
        ring_scratch.append(pltpu.SemaphoreType.DMA((len(a_ops), RING_SLOTS)))
    else:
        a_specs = [pl.BlockSpec(bs, im) for _, bs, im in a_ops]

    def w_spec(layer, rows, rb, co):
        if transposed:
            return pl.BlockSpec((None, tn, rows), lambda j, i: (layer, j + co, rb))
        return pl.BlockSpec((None, rows, tn), lambda j, i: (layer, rb, j + co))

    w_specs = [w_spec(layer, rows, rb, co) for _, layer, rows, rb, co in w_ops]
    cast = w_ops[0][0].dtype != BF16
    scratch = []
    if cast:
        scratch = [pltpu.VMEM((tn, rows) if transposed else (rows, tn), BF16) for _, _, rows, _, _ in w_ops]
    return pl.pallas_call(
        functools.partial(_mm_body, n_a=len(a_ops), n_w=len(w_ops), n_extra=len(extra_ops),
                          epilogue=epilogue, cast=cast, transposed=transposed, ring=ring_args,
                          side=side is not None),
        grid=(n // tn, n_m),
        in_specs=a_specs + w_specs + e_specs,
        out_specs=out_specs,
        out_shape=out_shape,
        scratch_shapes=scratch + ring_scratch,
        compiler_params=_params("arbitrary", "arbitrary"),
        name=name,
    )(*[op[0] for op in a_ops], *[op[0] for op in w_ops], *[op[0] for op in extra_ops], *side_ops)


def _rows(tm, k, col_block=0):
    return (tm, k), lambda j, i: (i, col_block)


def _as3d(w):
    return w if w.ndim == 3 else w[None]


def _matmul(a, w, *, layer=0, n=None, out_dtype, tm, tn, name, transposed=False, ring=False):
    w = _as3d(w)
    m, k = a.shape
    if n is None:
        n = w.shape[1] if transposed else w.shape[2]
    return _mm(name, _ep_plain, [(a, *_rows(tm, k))], [(w, layer, k, 0, 0)], [],
               m=m, n=n, tm=tm, tn=tn, out_dtype=out_dtype, transposed=transposed, ring=ring)


def _matmul_mul(a, w, table, *, out_dtype, tm, tn, name):
    w = _as3d(w)
    m, k = a.shape
    c = table.shape[1]
    return _mm(name, _ep_mul, [(a, *_rows(tm, k))], [(w, 0, k, 0, 0)], [(table, (tm, c), lambda j, i: (i, 0))],
               m=m, n=w.shape[2], tm=tm, tn=tn, out_dtype=out_dtype)


def _matmul_res(a, w, res, *, layer, tm, tn, name):
    m, k = a.shape
    return _mm(name, _ep_residual, [(a, *_rows(tm, k))], [(w, layer, k, 0, 0)],
               [(res, (tm, tn), lambda j, i: (i, j))], m=m, n=w.shape[2], tm=tm, tn=tn, out_dtype=F32)


def _matmul_res2(a1, a2, w, res, *, layer, tm, tn, name, ring=False):
    m, k1 = a1.shape
    k2 = a2.shape[1]
    assert k1 % k2 == 0
    return _mm(name, _ep_residual, [(a1, *_rows(tm, k1)), (a2, *_rows(tm, k2))],
               [(w, layer, k1, 0, 0), (w, layer, k2, k1 // k2, 0)],
               [(res, (tm, tn), lambda j, i: (i, j))], m=m, n=w.shape[2], tm=tm, tn=tn, out_dtype=F32, ring=ring)


def _matmul_swiglu(a, w_gu, *, layer, tm, tn, name, ring=False, side=None):
    m, k = a.shape
    f = w_gu.shape[2] // 2
    return _mm(name, _ep_swiglu, [(a, *_rows(tm, k))], [(w_gu, layer, k, 0, 0), (w_gu, layer, k, 0, f // tn)], [],
               m=m, n=f, tm=tm, tn=tn, out_dtype=BF16, ring=ring, side=side)


def _krope_body(x_ref, t_ref, o_ref):
    y = x_ref[...] * t_ref[...]
    o_ref[...] = (y + pltpu.roll(y, MLA_ROPE_DIM, 1)).astype(o_ref.dtype)


def _krope(p, table, *, col_block, name="mla_krope"):
    t = p.shape[0]
    tm = _pick(t, (1024, 512, 256, 128, 8))
    w = 2 * MLA_ROPE_DIM
    return pl.pallas_call(
        _krope_body,
        grid=(t // tm,),
        in_specs=[pl.BlockSpec((tm, w), lambda i: (i, col_block)),
                  pl.BlockSpec((tm, w), lambda i: (i, 0))],
        out_specs=pl.BlockSpec((tm, w), lambda i: (i, 0)),
        out_shape=jax.ShapeDtypeStruct((t, w), BF16),
        compiler_params=_params("parallel"),
        name=name,
    )(p, table)


def _flash_body(q_ref, kn_ref, v_ref, kr_ref, o_ref, k_scr, *, tk):
    @pl.when(pl.program_id(2) == 0)
    def _():
        k_scr[:, :MLA_NOPE_DIM] = kn_ref[...]
        k_scr[:, MLA_NOPE_DIM:] = kr_ref[...]

    q = q_ref[...]
    tq = q.shape[0]
    s_len = k_scr.shape[0]
    m = jnp.full((tq, 1), -jnp.inf, F32)
    l = jnp.zeros((tq, 1), F32)
    acc = jnp.zeros((tq, MLA_V_DIM), F32)
    for c in range(s_len // tk):
        s = _dot_nt(q, k_scr[c * tk:(c + 1) * tk, :])
        m_new = jnp.maximum(m, jnp.max(s, axis=-1, keepdims=True))
        alpha = jnp.exp2(m - m_new)
        p = jnp.exp2(s - m_new)
        l = alpha * l + jnp.sum(p, axis=-1, keepdims=True)
        acc = alpha * acc + _dot(p.astype(BF16), v_ref[c * tk:(c + 1) * tk, :])
        m = m_new
    o_ref[...] = (acc / l).astype(o_ref.dtype)


def _flash_attention(q, kv, kr, *, batch, heads, name="mla_flash"):
    t = q.shape[0]
    s_len = t // batch
    tq = _pick(s_len, (2048, 1024, 512, 256, 128))
    tk = _pick(s_len, (1024, 512, 256, 128))
    nq = s_len // tq
    return pl.pallas_call(
        functools.partial(_flash_body, tk=tk),
        grid=(batch, heads, nq),
        in_specs=[pl.BlockSpec((tq, MLA_HEAD_COLS), lambda b, h, i: (b * nq + i, h)),
                  pl.BlockSpec((s_len, MLA_NOPE_DIM), lambda b, h, i: (b, 2 * h)),
                  pl.BlockSpec((s_len, MLA_V_DIM), lambda b, h, i: (b, 2 * h + 1)),
                  pl.BlockSpec((s_len, 2 * MLA_ROPE_DIM), lambda b, h, i: (b, 0))],
        out_specs=pl.BlockSpec((tq, MLA_V_DIM), lambda b, h, i: (b * nq + i, h)),
        out_shape=jax.ShapeDtypeStruct((t, heads * MLA_V_DIM), BF16),
        scratch_shapes=[pltpu.VMEM((s_len, MLA_HEAD_COLS), BF16)],
        compiler_params=_params("parallel", "parallel", "arbitrary"),
        name=name,
    )(q, kv, kv, kr)


def _memattn_body(*refs, scale):
    q_refs = refs[:MEM_HEADS]
    mk_ref, mv_ref, o_ref = refs[MEM_HEADS:]
    dh = q_refs[0].shape[1]
    for h in range(MEM_HEADS):
        q = (q_refs[h][...].astype(F32) * scale).astype(BF16)
        s = _dot_nt(q, mk_ref[:, h * dh:(h + 1) * dh])
        p = jnp.exp(s - jnp.max(s, axis=-1, keepdims=True))
        l = jnp.sum(p, axis=-1, keepdims=True)
        o = _dot((p / l).astype(BF16), mv_ref[:, h * dh:(h + 1) * dh])
        o_ref[:, h * dh:(h + 1) * dh] = o.astype(o_ref.dtype)


def _memory_attention(p, mkv, *, batch, q_col, width, name="mem_attn"):
    t = p.shape[0]
    s_len = t // batch
    n_mem = mkv.shape[0] // batch
    dh = width // MEM_HEADS
    tq = _pick(s_len, (512, 256, 128))
    nq = s_len // tq
    q_specs = [pl.BlockSpec((tq, dh), functools.partial(lambda b, i, h: (b * nq + i, q_col // dh + h), h=h))
               for h in range(MEM_HEADS)]
    return pl.pallas_call(
        functools.partial(_memattn_body, scale=dh ** -0.5),
        grid=(batch, nq),
        in_specs=q_specs + [pl.BlockSpec((n_mem, width), lambda b, i: (b, 0)),
                            pl.BlockSpec((n_mem, width), lambda b, i: (b, 1))],
        out_specs=pl.BlockSpec((tq, width), lambda b, i: (b * nq + i, 0)),
        out_shape=jax.ShapeDtypeStruct((t, width), BF16),
        compiler_params=_params("parallel", "parallel"),
        name=name,
    )(*([p] * MEM_HEADS), mkv, mkv)


def _gates_body(w_ref, x_ref, b_ref, f_ref, o_ref):
    g = _dot_nt(w_ref[...].astype(BF16), x_ref[...]) + b_ref[...]
    igate = IGATE_CAP * jnp.tanh(g / IGATE_CAP)
    log_fgate = jnp.minimum(g, 0.0) - jnp.log1p(jnp.exp(-jnp.abs(g)))
    o_ref[...] = jnp.where(f_ref[...] > 0.5, log_fgate, igate) * math.log2(math.e)


def _mlstm_gates(w_t, hn, bias, *, heads, name="mlstm_gates"):
    rows, k = w_t.shape
    t = hn.shape[0]
    tn = _pick(t, (1024, 512, 256, 128))
    is_f = (jnp.arange(rows) // heads) % 2 == 1
    return pl.pallas_call(
        _gates_body,
        grid=(t // tn,),
        in_specs=[pl.BlockSpec((rows, k), lambda i: (0, 0)),
                  pl.BlockSpec((tn, k), lambda i: (i, 0)),
                  pl.BlockSpec((rows, 1), lambda i: (0, 0)),
                  pl.BlockSpec((rows, 1), lambda i: (0, 0))],
        out_specs=pl.BlockSpec((rows, tn), lambda i: (0, i)),
        out_shape=jax.ShapeDtypeStruct((rows, t), F32),
        compiler_params=_params("parallel"),
        name=name,
    )(w_t, hn, bias.reshape(rows, 1).astype(F32), is_f.astype(F32).reshape(rows, 1))


def _conv_body(x_ref, prev_ref, next_ref, w_ref, o_ref, *, blocks_per_seq, q_blocks, q_scale):
    i = pl.program_id(0)
    j = pl.program_id(1)
    x = x_ref[...].astype(F32)
    ts = x.shape[0]
    pos = i % blocks_per_seq
    halo = prev_ref.shape[0]
    prev_row = jnp.where(pos == 0, 0.0, prev_ref[...].astype(F32)[halo - 1:halo, :])
    next_row = jnp.where(pos == blocks_per_seq - 1, 0.0, next_ref[...].astype(F32)[0:1, :])
    row = lax.broadcasted_iota(jnp.int32, x.shape, 0)
    x_prev = jnp.where(row == 0, prev_row, pltpu.roll(x, 1, 0))
    x_next = jnp.where(row == ts - 1, next_row, pltpu.roll(x, ts - 1, 0))
    y = x_prev * w_ref[0:1, :] + x * w_ref[1:2, :] + x_next * w_ref[2:3, :]
    y = y * jax.nn.sigmoid(y)
    scale = jnp.where(j < q_blocks, q_scale, 1.0)
    o_ref[...] = (y * scale).astype(o_ref.dtype)


def _mlstm_conv(p, conv_w, *, batch, width, q_width, name="mlstm_conv"):
    t = p.shape[0]
    s_len = t // batch
    ts = _pick(s_len, (512, 256, 128))
    tc = _pick(q_width, (1536, 1024, 512, 256, 128))
    halo = 8 * (4 // p.dtype.itemsize)
    hb = ts // halo
    return pl.pallas_call(
        functools.partial(_conv_body, blocks_per_seq=s_len // ts, q_blocks=q_width // tc,
                          q_scale=MLSTM_QK_DIM ** -0.5),
        grid=(t // ts, width // tc),
        in_specs=[pl.BlockSpec((ts, tc), lambda i, j: (i, j)),
                  pl.BlockSpec((halo, tc), lambda i, j: (jnp.maximum(i * hb - 1, 0), j)),
                  pl.BlockSpec((halo, tc), lambda i, j: (jnp.minimum((i + 1) * hb, t // halo - 1), j)),
                  pl.BlockSpec((3, tc), lambda i, j: (0, j))],
        out_specs=pl.BlockSpec((ts, tc), lambda i, j: (i, j)),
        out_shape=jax.ShapeDtypeStruct((t, width), BF16),
        compiler_params=_params("parallel", "parallel"),
        name=name,
    )(p, p, p, conv_w.astype(F32))


def _mlstm_chunk(q, k, v, ig, lf, c_scr, n_scr, m_scr, *, sign):
    ln = q.shape[0]
    t_idx = lax.broadcasted_iota(jnp.int32, (ln, ln), 0)
    j_idx = lax.broadcasted_iota(jnp.int32, (ln, ln), 1)
    mask = (j_idx - t_idx) * sign <= 0
    eye = j_idx == t_idx
    lf_b = jnp.broadcast_to(lf, (ln, ln))
    ig_b = jnp.broadcast_to(ig, (ln, ln))
    cs_col = jnp.sum(jnp.where(mask, lf_b, 0.0), axis=1, keepdims=True)
    cs_row = jnp.sum(jnp.where(eye, cs_col, 0.0), axis=0, keepdims=True)
    ig_col = jnp.sum(jnp.where(eye, ig_b, 0.0), axis=1, keepdims=True)
    m_prev = m_scr[...]

    dmat = jnp.where(mask, cs_col - cs_row + ig, -jnp.inf)
    inter = cs_col + m_prev
    m_row = jnp.maximum(jnp.max(dmat, axis=1, keepdims=True), inter)
    s = _dot_nt(q, k) * jnp.exp2(dmat - m_row)
    w_inter = jnp.exp2(inter - m_row)
    num = _dot(s.astype(BF16), v) + w_inter * _dot(q, c_scr[...].astype(BF16))
    qn = jnp.sum(q.astype(F32) * n_scr[...], axis=1, keepdims=True)
    den = jnp.sum(s, axis=1, keepdims=True) + w_inter * qn
    h = num / jnp.maximum(jnp.abs(den), jnp.exp2(-m_row))

    b_end = jnp.sum(lf, axis=1, keepdims=True)
    w_tok_row = b_end - cs_row + ig
    w_tok_col = b_end - cs_col + ig_col
    m_new = jnp.maximum(b_end + m_prev, jnp.max(w_tok_row, axis=1, keepdims=True))
    decay = jnp.exp2(b_end + m_prev - m_new)
    wk = k.astype(F32) * jnp.exp2(w_tok_col - m_new)
    c_scr[...] = decay * c_scr[...] + _dot_tn(wk.astype(BF16), v)
    n_scr[...] = decay * n_scr[...] + jnp.sum(wk, axis=0, keepdims=True)
    m_scr[...] = m_new
    return h


def _mlstm_scan_body(*refs, hps, nc):
    q_ref, k_ref, v_ref = refs[:3]
    gate_refs = refs[3:3 + 2 * hps]
    o_pre_ref, g_ref, out_ref = refs[3 + 2 * hps:6 + 2 * hps]
    hf_scr = refs[6 + 2 * hps]
    state = refs[7 + 2 * hps:]
    dk, dv = MLSTM_QK_DIM, MLSTM_V_DIM
    ln = q_ref.shape[0]
    d = pl.program_id(1)
    c = pl.program_id(2)

    @pl.when(c == 0)
    def _():
        for idx in range(0, len(state), 3):
            c_scr, n_scr, m_scr = state[idx:idx + 3]
            c_scr[...] = jnp.zeros_like(c_scr)
            n_scr[...] = jnp.zeros_like(n_scr)
            m_scr[...] = jnp.full_like(m_scr, STAB_INIT)

    hs = [_mlstm_chunk(q_ref[:, hh * dk:(hh + 1) * dk], k_ref[:, hh * dk:(hh + 1) * dk],
                       v_ref[:, hh * dv:(hh + 1) * dv], gate_refs[2 * hh][0], gate_refs[2 * hh + 1][0],
                       *state[3 * hh:3 * hh + 3], sign=1 - 2 * d) for hh in range(hps)]

    @pl.when(d == 0)
    def _():
        rows = pl.ds(pl.multiple_of(c * ln, ln), ln)
        for hh in range(hps):
            hf_scr[rows, hh * dv:(hh + 1) * dv] = hs[hh]

    @pl.when(d == 1)
    def _():
        rows = pl.ds(pl.multiple_of((nc - 1 - c) * ln, ln), ln)
        for hh in range(hps):
            cols = slice(hh * dv, (hh + 1) * dv)
            h = hf_scr[rows, cols] + hs[hh]
            ms = jnp.mean(h * h, axis=-1, keepdims=True)
            y = h * lax.rsqrt(ms + EPS) * g_ref[:, cols]
            out_ref[:, cols] = (y * jax.nn.sigmoid(o_pre_ref[:, cols].astype(F32))).astype(out_ref.dtype)


def _mlstm_scan(qk, p, gates, head_norm, *, batch, heads, v_col, o_col, name="mlstm_scan"):
    t = qk.shape[0]
    s_len = t // batch
    ln = gates.shape[-1]
    nc = s_len // ln
    dk, dv = MLSTM_QK_DIM, MLSTM_V_DIM
    hps = next(n for n in (3, 2, 1) if heads % n == 0)
    groups = heads // hps

    def chunk(d, c):
        return c + d * (nc - 1 - 2 * c)

    def row_block(g, d, c):
        return (g // groups) * nc + chunk(d, c)

    def out_row_block(g, d, c):
        return (g // groups) * nc + nc - 1 - d * c

    def gate_block(kind, hh):
        def index(g, d, c):
            head = (g % groups) * hps + hh
            return (((2 * d + kind) * heads + head) * batch + g // groups) * nc + chunk(d, c), 0, 0
        return index

    wide = hps * dv
    in_specs = [pl.BlockSpec((ln, hps * dk), lambda g, d, c: (row_block(g, d, c), g % groups)),
                pl.BlockSpec((ln, hps * dk), lambda g, d, c: (row_block(g, d, c), groups + g % groups)),
                pl.BlockSpec((ln, wide), lambda g, d, c: (row_block(g, d, c), v_col // wide + g % groups))]
    for hh in range(hps):
        in_specs += [pl.BlockSpec((1, 1, ln), gate_block(0, hh)), pl.BlockSpec((1, 1, ln), gate_block(1, hh))]
    in_specs += [pl.BlockSpec((ln, wide), lambda g, d, c: (out_row_block(g, d, c), o_col // wide + g % groups)),
                 pl.BlockSpec((1, wide), lambda g, d, c: (0, g % groups))]
    state = [pltpu.VMEM((dk, dv), F32), pltpu.VMEM((1, dk), F32), pltpu.VMEM((1, 1), F32)]
    return pl.pallas_call(
        functools.partial(_mlstm_scan_body, hps=hps, nc=nc),
        grid=(batch * groups, 2, nc),
        in_specs=in_specs,
        out_specs=pl.BlockSpec((ln, wide), lambda g, d, c: (out_row_block(g, d, c), g % groups)),
        out_shape=jax.ShapeDtypeStruct((t, heads * dv), BF16),
        scratch_shapes=[pltpu.VMEM((s_len, wide), F32)] + state * hps,
        compiler_params=_params("parallel", "arbitrary", "arbitrary"),
        name=name,
    )(qk, qk, p, *([gates, gates] * hps), p, head_norm.reshape(1, heads * dv).astype(F32))


def _swap_halves(w, axis):
    half = w.shape[axis] // 2
    lo, hi = jnp.split(w, [half], axis=axis)
    return jnp.concatenate([hi, lo], axis=axis)


def _prep_mla_weights(w_in_t, w_uq, q_rank, kv_rank, heads):
    a1 = q_rank + kv_rank
    a2 = a1 + MLA_ROPE_DIM
    w_kr = w_in_t[a1:a2]
    w_in_p = jnp.concatenate([w_in_t[:a1], w_in_t[a2:], w_kr, _swap_halves(w_kr, 0)], axis=0).astype(BF16)
    wq = w_uq.reshape(q_rank, heads, MLA_NOPE_DIM + MLA_ROPE_DIM)
    rope_cols = wq[..., MLA_NOPE_DIM:]
    w_uq_p = jnp.concatenate([wq[..., :MLA_NOPE_DIM], rope_cols, _swap_halves(rope_cols, 2)], axis=-1)
    return w_in_p, w_uq_p.reshape(q_rank, heads * MLA_HEAD_COLS).astype(BF16)


def _rope_tables(positions):
    inv_freq = 1.0 / (ROPE_THETA ** (jnp.arange(0, MLA_ROPE_DIM, 2, dtype=F32) / MLA_ROPE_DIM))
    ang = positions.astype(F32).reshape(-1, 1) * inv_freq
    cos, sin = jnp.cos(ang), jnp.sin(ang)
    k_tab = jnp.concatenate([cos, cos, -sin, sin], axis=1)
    scale = (MLA_NOPE_DIM + MLA_ROPE_DIM) ** -0.5 * math.log2(math.e)
    q_tab = jnp.concatenate([jnp.ones((ang.shape[0], MLA_NOPE_DIM), F32), k_tab], axis=1) * scale
    return q_tab, k_tab


def kernel(x, positions, mem, attn_norm, ffn_norm, mem_norm, final_norm, mla_w_in, mla_q_norm, mla_kv_norm,
           mla_w_uq, mla_w_ukv, mlstm_w_in, mlstm_conv_w, mlstm_gate_b, mlstm_head_norm, w_mem_kv, w_out,
           w_gu, w_down):
    batch, s_len, d_model = x.shape
    depth = attn_norm.shape[0]
    n_mem = mem.shape[1]
    t = batch * s_len
    mem_width = d_model // 4
    main_width = d_model - mem_width
    mla_heads = main_width // MLA_V_DIM
    mlstm_heads = main_width // MLSTM_V_DIM
    qk_width = mlstm_heads * MLSTM_QK_DIM
    v_width = mlstm_heads * MLSTM_V_DIM
    q_rank = mla_q_norm.shape[1]
    kv_rank = mla_kv_norm.shape[1]
    d_ff = w_down.shape[1]

    tm = _pick(t, (1024, 512, 256, 128))
    q_tab, k_tab = _rope_tables(positions)
    mem_n = _rmsnorm(mem.reshape(batch * n_mem, d_model), mem_norm, width=d_model, name="mem_norm")
    mla_w_in_t = jnp.swapaxes(mla_w_in, 1, 2)
    mlstm_w_in_t = jnp.swapaxes(mlstm_w_in, 1, 2)
    h = x.reshape(t, d_model)

    for i in range(depth):
        j = i // N_MIXERS
        hn = _rmsnorm(h, attn_norm[i], width=d_model, name="attn_norm")
        mkv = _matmul(mem_n, w_mem_kv, layer=i, out_dtype=BF16, tm=_pick(batch * n_mem, (512, 256, 128)),
                      tn=_pick(2 * mem_width, (512, 256)), name="mem_kv")
        if i % N_MIXERS == 0:
            w_in_p, w_uq_p = _prep_mla_weights(mla_w_in_t[j], mla_w_uq[j], q_rank, kv_rank, mla_heads)
            n_in = w_in_p.shape[0]
            p = _matmul(hn, w_in_p, out_dtype=F32, tm=tm, tn=_pick(n_in, (896, 768, 640, 512, 384, 256, 128)),
                        name="mla_in_proj", transposed=True, ring=True)
            cq = _rmsnorm(p, mla_q_norm[j], width=q_rank, col_block=0, name="mla_q_norm")
            ckv = _rmsnorm(p, mla_kv_norm[j], width=kv_rank, col_block=q_rank // kv_rank, name="mla_kv_norm")
            head_cols = mla_heads * MLA_HEAD_COLS
            tm_small_k = _pick(t, (2048, 1024, 512, 256, 128))
            q = _matmul_mul(cq, w_uq_p, q_tab, out_dtype=BF16, tm=tm_small_k,
                            tn=_pick(head_cols, (1024, 768, 512, 256)), name="mla_q_proj")
            kv = _matmul(ckv, mla_w_ukv, layer=j, out_dtype=BF16, tm=tm_small_k,
                         tn=_pick(head_cols, (1536, 1024, 768, 512, 256)), name="mla_kv_proj")
            q_mem_src, q_mem_col = p, q_rank + kv_rank
            kr = _krope(p, k_tab, col_block=(q_mem_col + mem_width) // (2 * MLA_ROPE_DIM))
            main = _flash_attention(q, kv, kr, batch=batch, heads=mla_heads)
        else:
            g_col = 2 * qk_width + 2 * v_width
            w_gates_t = mlstm_w_in_t[j, g_col:g_col + 4 * mlstm_heads]
            w_q_mem_t = mlstm_w_in_t[j, g_col + 4 * mlstm_heads:]
            p = _matmul(hn, mlstm_w_in_t, layer=j, n=g_col, out_dtype=BF16, tm=tm, tn=_pick(g_col, (512, 256, 128)),
                        name="mlstm_in_proj", transposed=True, ring=True)
            q_mem_src = _matmul(hn, w_q_mem_t, out_dtype=BF16, tm=tm, tn=_pick(mem_width, (512, 256, 128)),
                                name="mlstm_qmem_proj", transposed=True)
            q_mem_col = 0
            gates = _mlstm_gates(w_gates_t, hn, mlstm_gate_b[j], heads=mlstm_heads)
            gates = gates.reshape(-1, 1, min(MLSTM_CHUNK, s_len))
            qk = _mlstm_conv(p, mlstm_conv_w[j], batch=batch, width=2 * qk_width, q_width=qk_width)
            main = _mlstm_scan(qk, p, gates, mlstm_head_norm[j], batch=batch, heads=mlstm_heads,
                               v_col=2 * qk_width, o_col=2 * qk_width + v_width)
        mem_out = _memory_attention(q_mem_src, mkv, batch=batch, q_col=q_mem_col, width=mem_width)
        h = _matmul_res2(main, mem_out, w_out, h, layer=i, tm=tm, tn=_pick(d_model, (512, 256)), name="out_proj",
                         ring=True)
        hn = _rmsnorm(h, ffn_norm[i], width=d_model, name="ffn_norm")
        act, w_down_bf16 = _matmul_swiglu(hn, w_gu, layer=i, tm=tm, tn=_pick(d_ff, (256, 128)), name="ffn_up",
                                          ring=True, side=(w_down, i))
        h = _matmul_res(act, w_down_bf16[None], h, layer=0, tm=_pick(t, (512, 256, 128)),
                        tn=_pick(d_model, (512, 256)), name="ffn_down")
    out = _rmsnorm(h, final_norm, width=d_model, out_dtype=x.dtype, name="final_norm")
    return out.reshape(batch, s_len, d_model)
```

```python
import functools
import math

import jax
import jax.numpy as jnp
from jax import lax
from jax.experimental import pallas as pl
from jax.experimental.pallas import tpu as pltpu

F32 = jnp.float32
BF16 = jnp.bfloat16

EPS = 1e-6
ROPE_THETA = 10000.0
MLA_NOPE_DIM = 128
MLA_ROPE_DIM = 64
MLA_V_DIM = 128
MLA_HEAD_COLS = 2 * MLA_NOPE_DIM
MLSTM_V_DIM = 512
MLSTM_QK_DIM = MLSTM_V_DIM // 2
MLSTM_CHUNK = 512
MEM_HEADS = 4
IGATE_CAP = 15.0
STAB_INIT = -1e30
N_MIXERS = 2

V7X_SUBLANES = 8
V7X_VMEM_BYTES = 64 * 1024 * 1024
VMEM_LIMIT_BYTES = V7X_VMEM_BYTES - 8 * 1024 * 1024


def _params(*semantics):
    return pltpu.CompilerParams(dimension_semantics=semantics, vmem_limit_bytes=VMEM_LIMIT_BYTES)


def _tile_rows(dtype):
    return V7X_SUBLANES * (4 // jnp.dtype(dtype).itemsize)


def _pick(n, candidates):
    for c in candidates:
        if n % c == 0:
            return c
    return n


def _dot(a, b):
    return jnp.dot(a, b, preferred_element_type=F32)


def _dot_nt(a, b):
    return lax.dot_general(a, b, (((1,), (1,)), ((), ())), preferred_element_type=F32)


def _dot_tn(a, b):
    return lax.dot_general(a, b, (((0,), (0,)), ((), ())), preferred_element_type=F32)


def _rmsnorm_body(x_ref, g_ref, o_ref):
    x = x_ref[...].astype(F32)
    ms = jnp.mean(x * x, axis=-1, keepdims=True)
    o_ref[...] = (x * lax.rsqrt(ms + EPS) * g_ref[...]).astype(o_ref.dtype)


def _rmsnorm(x, g, *, width, col_block=0, out_dtype=BF16, name="rmsnorm"):
    t = x.shape[0]
    tm = _pick(t, (512, 256, 128, 64, 8))
    return pl.pallas_call(
        _rmsnorm_body,
        grid=(t // tm,),
        in_specs=[pl.BlockSpec((tm, width), lambda i: (i, col_block)),
                  pl.BlockSpec((1, width), lambda i: (0, 0))],
        out_specs=pl.BlockSpec((tm, width), lambda i: (i, 0)),
        out_shape=jax.ShapeDtypeStruct((t, width), out_dtype),
        compiler_params=_params("parallel"),
        name=name,
    )(x, g.reshape(1, width).astype(F32))


def _ep_plain(mm, a, w, extra, o_ref):
    o_ref[...] = mm(a[0], w[0][...]).astype(o_ref.dtype)


def _ep_mul(mm, a, w, extra, o_ref):
    table = extra[0][...]
    reps = o_ref.shape[1] // table.shape[1]
    if reps > 1:
        table = jnp.concatenate([table] * reps, axis=1)
    o_ref[...] = (mm(a[0], w[0][...]) * table).astype(o_ref.dtype)


def _ep_residual(mm, a, w, extra, o_ref):
    acc = mm(a[0], w[0][...])
    for a_val, w_ref in zip(a[1:], w[1:]):
        acc = acc + mm(a_val, w_ref[...])
    o_ref[...] = extra[0][...] + acc


def _ep_swiglu(mm, a, w, extra, o_ref):
    g = mm(a[0], w[0][...])
    u = mm(a[0], w[1][...])
    o_ref[...] = (g * jax.nn.sigmoid(g) * u).astype(o_ref.dtype)


RING_SLOTS = 3


def _ring_fetch(a_hbm, bufs, sems, *, tm, n_m, total):
    step = pl.program_id(0) * n_m + pl.program_id(1)

    def copy(op, st):
        row = (st % n_m) * tm
        if not isinstance(row, int):
            row = pl.multiple_of(row, tm)
        slot = st % RING_SLOTS
        return pltpu.make_async_copy(a_hbm[op].at[pl.ds(row, tm), :], bufs[op].at[slot], sems.at[op, slot])

    @pl.when(step == 0)
    def _():
        for op in range(len(a_hbm)):
            for st in range(min(RING_SLOTS - 1, total)):
                copy(op, st).start()

    @pl.when(step + (RING_SLOTS - 1) < total)
    def _():
        for op in range(len(a_hbm)):
            copy(op, step + (RING_SLOTS - 1)).start()

    for op in range(len(a_hbm)):
        copy(op, step).wait()
    slot = step % RING_SLOTS
    return [buf[slot] for buf in bufs]


def _mm_body(*refs, n_a, n_w, n_extra, epilogue, cast, transposed, ring, side):
    a = refs[:n_a]
    w = refs[n_a:n_a + n_w]
    extra = refs[n_a + n_w:n_a + n_w + n_extra]
    n_in = n_a + n_w + n_extra
    o_ref = refs[n_in]
    scratch = refs[n_in + 1:]
    if side:
        refs[n_in + 2][...] = refs[n_in][...].astype(BF16)
        o_ref = refs[n_in + 1]
        scratch = refs[n_in + 3:]
    if cast:
        w_scratch, scratch = scratch[:n_w], scratch[n_w:]

        @pl.when(pl.program_id(1) == 0)
        def _():
            for w_ref, s_ref in zip(w, w_scratch):
                s_ref[...] = w_ref[...].astype(BF16)

        w = w_scratch
    if ring is None:
        a_vals = [a_ref[...] for a_ref in a]
    else:
        a_vals = _ring_fetch(a, scratch[:n_a], scratch[n_a], **ring)
    epilogue(_dot_nt if transposed else _dot, a_vals, w, extra, o_ref)


def _mm(name, epilogue, a_ops, w_ops, extra_ops, *, m, n, tm, tn, out_dtype, transposed=False, ring=False,
        side=None):
    e_specs = [pl.BlockSpec(bs, im) for _, bs, im in extra_ops]
    n_m = m // tm
    out_specs = pl.BlockSpec((tm, tn), lambda j, i: (i, j))
    out_shape = jax.ShapeDtypeStruct((m, n), out_dtype)
    side_ops = []
    if side is not None:
        side_arr, side_layer = side
        _, side_rows, side_cols = side_arr.shape
        slab = side_rows // ((n // tn) * n_m)
        assert slab * (n // tn) * n_m == side_rows and slab % _tile_rows(BF16) == 0
        e_specs.append(pl.BlockSpec((None, slab, side_cols), lambda j, i: (side_layer, j * n_m + i, 0)))
        out_specs = [out_specs, pl.BlockSpec((slab, side_cols), lambda j, i: (j * n_m + i, 0))]
        out_shape = [out_shape, jax.ShapeDtypeStruct((side_rows, side_cols), BF16)]
        side_ops = [side_arr]
    ring_args = None
    ring_scratch = []
    if ring:
        a_specs = [pl.BlockSpec(memory_space=pl.ANY) for _ in a_ops]
        ring_args = dict(tm=tm, n_m=n_m, total=(n // tn) * n_m)
        ring_scratch = [pltpu.VMEM((RING_SLOTS,) + tuple(bs), arr.dtype) for arr, bs, _ in a_ops]
        ring_scratch.append(pltpu.SemaphoreType.DMA((len(a_ops), RING_SLOTS)))
    else:
        a_specs = [pl.BlockSpec(bs, im) for _, bs, im in a_ops]

    def w_spec(layer, rows, rb, co):
        if transposed:
            return pl.BlockSpec((None, tn, rows), lambda j, i: (layer, j + co, rb))
        return pl.BlockSpec((None, rows, tn), lambda j, i: (layer, rb, j + co))

    w_specs = [w_spec(layer, rows, rb, co) for _, layer, rows, rb, co in w_ops]
    cast = w_ops[0][0].dtype != BF16
    scratch = []
    if cast:
        scratch = [pltpu.VMEM((tn, rows) if transposed else (rows, tn), BF16) for _, _, rows, _, _ in w_ops]
    return pl.pallas_call(
        functools.partial(_mm_body, n_a=len(a_ops), n_w=len(w_ops), n_extra=len(extra_ops),
                          epilogue=epilogue, cast=cast, transposed=transposed, ring=ring_args,
                          side=side is not None),
        grid=(n // tn, n_m),
        in_specs=a_specs + w_specs + e_specs,
        out_specs=out_specs,
        out_shape=out_shape,
        scratch_shapes=scratch + ring_scratch,
        compiler_params=_params("arbitrary", "arbitrary"),
        name=name,
    )(*[op[0] for op in a_ops], *[op[0] for op in w_ops], *[op[0] for op in extra_ops], *side_ops)


def _rows(tm, k, col_block=0):
    return (tm, k), lambda j, i: (i, col_block)


def _as3d(w):
    return w if w.ndim == 3 else w[None]


def _matmul(a, w, *, layer=0, n=None, out_dtype, tm, tn, name, transposed=False, ring=False):
    w = _as3d(w)
    m, k = a.shape
    if n is None:
        n = w.shape[1] if transposed else w.shape[2]
    return _mm(name, _ep_plain, [(a, *_rows(tm, k))], [(w, layer, k, 0, 0)], [],
               m=m, n=n, tm=tm, tn=tn, out_dtype=out_dtype, transposed=transposed, ring=ring)


def _matmul_mul(a, w, table, *, out_dtype, tm, tn, name):
    w = _as3d(w)
    m, k = a.shape
    c = table.shape[1]
    return _mm(name, _ep_mul, [(a, *_rows(tm, k))], [(w, 0, k, 0, 0)], [(table, (tm, c), lambda j, i: (i, 0))],
               m=m, n=w.shape[2], tm=tm, tn=tn, out_dtype=out_dtype)


def _matmul_res(a, w, res, *, layer, tm, tn, name):
    m, k = a.shape
    return _mm(name, _ep_residual, [(a, *_rows(tm, k))], [(w, layer, k, 0, 0)],
               [(res, (tm, tn), lambda j, i: (i, j))], m=m, n=w.shape[2], tm=tm, tn=tn, out_dtype=F32)


def _matmul_res2(a1, a2, w, res, *, layer, tm, tn, name, ring=False):
    m, k1 = a1.shape
    k2 = a2.shape[1]
    assert k1 % k2 == 0
    return _mm(name, _ep_residual, [(a1, *_rows(tm, k1)), (a2, *_rows(tm, k2))],
               [(w, layer, k1, 0, 0), (w, layer, k2, k1 // k2, 0)],
               [(res, (tm, tn), lambda j, i: (i, j))], m=m, n=w.shape[2], tm=tm, tn=tn, out_dtype=F32, ring=ring)


def _matmul_swiglu(a, w_gu, *, layer, tm, tn, name, ring=False, side=None):
    m, k = a.shape
    f = w_gu.shape[2] // 2
    return _mm(name, _ep_swiglu, [(a, *_rows(tm, k))], [(w_gu, layer, k, 0, 0), (w_gu, layer, k, 0, f // tn)], [],
               m=m, n=f, tm=tm, tn=tn, out_dtype=BF16, ring=ring, side=side)


def _krope_body(x_ref, t_ref, o_ref):
    y = x_ref[...] * t_ref[...]
    o_ref[...] = (y + pltpu.roll(y, MLA_ROPE_DIM, 1)).astype(o_ref.dtype)


def _krope(p, table, *, col_block, name="mla_krope"):
    t = p.shape[0]
    tm = _pick(t, (1024, 512, 256, 128, 8))
    w = 2 * MLA_ROPE_DIM
    return pl.pallas_call(
        _krope_body,
        grid=(t // tm,),
        in_specs=[pl.BlockSpec((tm, w), lambda i: (i, col_block)),
                  pl.BlockSpec((tm, w), lambda i: (i, 0))],
        out_specs=pl.BlockSpec((tm, w), lambda i: (i, 0)),
        out_shape=jax.ShapeDtypeStruct((t, w), BF16),
        compiler_params=_params("parallel"),
        name=name,
    )(p, table)


def _flash_body(q_ref, kn_ref, v_ref, kr_ref, o_ref, k_scr, *, tk):
    @pl.when(pl.program_id(2) == 0)
    def _():
        k_scr[:, :MLA_NOPE_DIM] = kn_ref[...]
        k_scr[:, MLA_NOPE_DIM:] = kr_ref[...]

    q = q_ref[...]
    tq = q.shape[0]
    s_len = k_scr.shape[0]
    m = jnp.full((tq, 1), -jnp.inf, F32)
    l = jnp.zeros((tq, 1), F32)
    acc = jnp.zeros((tq, MLA_V_DIM), F32)
    for c in range(s_len // tk):
        s = _dot_nt(q, k_scr[c * tk:(c + 1) * tk, :])
        m_new = jnp.maximum(m, jnp.max(s, axis=-1, keepdims=True))
        alpha = jnp.exp2(m - m_new)
        p = jnp.exp2(s - m_new)
        l = alpha * l + jnp.sum(p, axis=-1, keepdims=True)
        acc = alpha * acc + _dot(p.astype(BF16), v_ref[c * tk:(c + 1) * tk, :])
        m = m_new
    o_ref[...] = (acc / l).astype(o_ref.dtype)


def _flash_attention(q, kv, kr, *, batch, heads, name="mla_flash"):
    t = q.shape[0]
    s_len = t // batch
    tq = _pick(s_len, (2048, 1024, 512, 256, 128))
    tk = _pick(s_len, (1024, 512, 256, 128))
    nq = s_len // tq
    return pl.pallas_call(
        functools.partial(_flash_body, tk=tk),
        grid=(batch, heads, nq),
        in_specs=[pl.BlockSpec((tq, MLA_HEAD_COLS), lambda b, h, i: (b * nq + i, h)),
                  pl.BlockSpec((s_len, MLA_NOPE_DIM), lambda b, h, i: (b, 2 * h)),
                  pl.BlockSpec((s_len, MLA_V_DIM), lambda b, h, i: (b, 2 * h + 1)),
                  pl.BlockSpec((s_len, 2 * MLA_ROPE_DIM), lambda b, h, i: (b, 0))],
        out_specs=pl.BlockSpec((tq, MLA_V_DIM), lambda b, h, i: (b * nq + i, h)),
        out_shape=jax.ShapeDtypeStruct((t, heads * MLA_V_DIM), BF16),
        scratch_shapes=[pltpu.VMEM((s_len, MLA_HEAD_COLS), BF16)],
        compiler_params=_params("parallel", "parallel", "arbitrary"),
        name=name,
    )(q, kv, kv, kr)


def _memattn_body(*refs, scale):
    q_refs = refs[:MEM_HEADS]
    mk_ref, mv_ref, o_ref = refs[MEM_HEADS:]
    dh = q_refs[0].shape[1]
    for h in range(MEM_HEADS):
        q = (q_refs[h][...].astype(F32) * scale).astype(BF16)
        s = _dot_nt(q, mk_ref[:, h * dh:(h + 1) * dh])
        p = jnp.exp(s - jnp.max(s, axis=-1, keepdims=True))
        l = jnp.sum(p, axis=-1, keepdims=True)
        o = _dot((p / l).astype(BF16), mv_ref[:, h * dh:(h + 1) * dh])
        o_ref[:, h * dh:(h + 1) * dh] = o.astype(o_ref.dtype)


def _memory_attention(p, mkv, *, batch, q_col, width, name="mem_attn"):
    t = p.shape[0]
    s_len = t // batch
    n_mem = mkv.shape[0] // batch
    dh = width // MEM_HEADS
    tq = _pick(s_len, (512, 256, 128))
    nq = s_len // tq
    q_specs = [pl.BlockSpec((tq, dh), functools.partial(lambda b, i, h: (b * nq + i, q_col // dh + h), h=h))
               for h in range(MEM_HEADS)]
    return pl.pallas_call(
        functools.partial(_memattn_body, scale=dh ** -0.5),
        grid=(batch, nq),
        in_specs=q_specs + [pl.BlockSpec((n_mem, width), lambda b, i: (b, 0)),
                            pl.BlockSpec((n_mem, width), lambda b, i: (b, 1))],
        out_specs=pl.BlockSpec((tq, width), lambda b, i: (b * nq + i, 0)),
        out_shape=jax.ShapeDtypeStruct((t, width), BF16),
        compiler_params=_params("parallel", "parallel"),
        name=name,
    )(*([p] * MEM_HEADS), mkv, mkv)


def _gates_body(w_ref, x_ref, b_ref, f_ref, o_ref):
    g = _dot_nt(w_ref[...].astype(BF16), x_ref[...]) + b_ref[...]
    igate = IGATE_CAP * jnp.tanh(g / IGATE_CAP)
    log_fgate = jnp.minimum(g, 0.0) - jnp.log1p(jnp.exp(-jnp.abs(g)))
    o_ref[...] = jnp.where(f_ref[...] > 0.5, log_fgate, igate) * math.log2(math.e)


def _mlstm_gates(w_t, hn, bias, *, heads, name="mlstm_gates"):
    rows, k = w_t.shape
    t = hn.shape[0]
    tn = _pick(t, (1024, 512, 256, 128))
    is_f = (jnp.arange(rows) // heads) % 2 == 1
    return pl.pallas_call(
        _gates_body,
        grid=(t // tn,),
        in_specs=[pl.BlockSpec((rows, k), lambda i: (0, 0)),
                  pl.BlockSpec((tn, k), lambda i: (i, 0)),
                  pl.BlockSpec((rows, 1), lambda i: (0, 0)),
                  pl.BlockSpec((rows, 1), lambda i: (0, 0))],
        out_specs=pl.BlockSpec((rows, tn), lambda i: (0, i)),
        out_shape=jax.ShapeDtypeStruct((rows, t), F32),
        compiler_params=_params("parallel"),
        name=name,
    )(w_t, hn, bias.reshape(rows, 1).astype(F32), is_f.astype(F32).reshape(rows, 1))


def _conv_body(x_ref, prev_ref, next_ref, w_ref, o_ref, *, blocks_per_seq, q_blocks, q_scale):
    i = pl.program_id(0)
    j = pl.program_id(1)
    x = x_ref[...].astype(F32)
    ts = x.shape[0]
    pos = i % blocks_per_seq
    halo = prev_ref.shape[0]
    prev_row = jnp.where(pos == 0, 0.0, prev_ref[...].astype(F32)[halo - 1:halo, :])
    next_row = jnp.where(pos == blocks_per_seq - 1, 0.0, next_ref[...].astype(F32)[0:1, :])
    row = lax.broadcasted_iota(jnp.int32, x.shape, 0)
    x_prev = jnp.where(row == 0, prev_row, pltpu.roll(x, 1, 0))
    x_next = jnp.where(row == ts - 1, next_row, pltpu.roll(x, ts - 1, 0))
    y = x_prev * w_ref[0:1, :] + x * w_ref[1:2, :] + x_next * w_ref[2:3, :]
    y = y * jax.nn.sigmoid(y)
    scale = jnp.where(j < q_blocks, q_scale, 1.0)
    o_ref[...] = (y * scale).astype(o_ref.dtype)


def _mlstm_conv(p, conv_w, *, batch, width, q_width, name="mlstm_conv"):
    t = p.shape[0]
    s_len = t // batch
    ts = _pick(s_len, (512, 256, 128))
    tc = _pick(q_width, (1536, 1024, 512, 256, 128))
    halo = _tile_rows(p.dtype)
    hb = ts // halo
    return pl.pallas_call(
        functools.partial(_conv_body, blocks_per_seq=s_len // ts, q_blocks=q_width // tc,
                          q_scale=MLSTM_QK_DIM ** -0.5),
        grid=(t // ts, width // tc),
        in_specs=[pl.BlockSpec((ts, tc), lambda i, j: (i, j)),
                  pl.BlockSpec((halo, tc), lambda i, j: (jnp.maximum(i * hb - 1, 0), j)),
                  pl.BlockSpec((halo, tc), lambda i, j: (jnp.minimum((i + 1) * hb, t // halo - 1), j)),
                  pl.BlockSpec((3, tc), lambda i, j: (0, j))],
        out_specs=pl.BlockSpec((ts, tc), lambda i, j: (i, j)),
        out_shape=jax.ShapeDtypeStruct((t, width), BF16),
        compiler_params=_params("parallel", "parallel"),
        name=name,
    )(p, p, p, conv_w.astype(F32))


def _mlstm_chunk(q, k, v, ig, lf, c_scr, n_scr, m_scr, *, sign):
    ln = q.shape[0]
    t_idx = lax.broadcasted_iota(jnp.int32, (ln, ln), 0)
    j_idx = lax.broadcasted_iota(jnp.int32, (ln, ln), 1)
    mask = (j_idx - t_idx) * sign <= 0
    eye = j_idx == t_idx
    lf_b = jnp.broadcast_to(lf, (ln, ln))
    ig_b = jnp.broadcast_to(ig, (ln, ln))
    cs_col = jnp.sum(jnp.where(mask, lf_b, 0.0), axis=1, keepdims=True)
    cs_row = jnp.sum(jnp.where(eye, cs_col, 0.0), axis=0, keepdims=True)
    ig_col = jnp.sum(jnp.where(eye, ig_b, 0.0), axis=1, keepdims=True)
    m_prev = m_scr[...]

    dmat = jnp.where(mask, cs_col - cs_row + ig, -jnp.inf)
    inter = cs_col + m_prev
    m_row = jnp.maximum(jnp.max(dmat, axis=1, keepdims=True), inter)
    s = _dot_nt(q, k) * jnp.exp2(dmat - m_row)
    w_inter = jnp.exp2(inter - m_row)
    num = _dot(s.astype(BF16), v) + w_inter * _dot(q, c_scr[...].astype(BF16))
    qn = jnp.sum(q.astype(F32) * n_scr[...], axis=1, keepdims=True)
    den = jnp.sum(s, axis=1, keepdims=True) + w_inter * qn
    h = num / jnp.maximum(jnp.abs(den), jnp.exp2(-m_row))

    b_end = jnp.sum(lf, axis=1, keepdims=True)
    w_tok_row = b_end - cs_row + ig
    w_tok_col = b_end - cs_col + ig_col
    m_new = jnp.maximum(b_end + m_prev, jnp.max(w_tok_row, axis=1, keepdims=True))
    decay = jnp.exp2(b_end + m_prev - m_new)
    wk = k.astype(F32) * jnp.exp2(w_tok_col - m_new)
    c_scr[...] = decay * c_scr[...] + _dot_tn(wk.astype(BF16), v)
    n_scr[...] = decay * n_scr[...] + jnp.sum(wk, axis=0, keepdims=True)
    m_scr[...] = m_new
    return h


def _mlstm_scan_body(*refs, hps, nc):
    q_ref, k_ref, v_ref = refs[:3]
    gate_refs = refs[3:3 + 2 * hps]
    o_pre_ref, g_ref, out_ref = refs[3 + 2 * hps:6 + 2 * hps]
    hf_scr = refs[6 + 2 * hps]
    state = refs[7 + 2 * hps:]
    dk, dv = MLSTM_QK_DIM, MLSTM_V_DIM
    ln = q_ref.shape[0]
    d = pl.program_id(1)
    c = pl.program_id(2)

    @pl.when(c == 0)
    def _():
        for idx in range(0, len(state), 3):
            c_scr, n_scr, m_scr = state[idx:idx + 3]
            c_scr[...] = jnp.zeros_like(c_scr)
            n_scr[...] = jnp.zeros_like(n_scr)
            m_scr[...] = jnp.full_like(m_scr, STAB_INIT)

    hs = [_mlstm_chunk(q_ref[:, hh * dk:(hh + 1) * dk], k_ref[:, hh * dk:(hh + 1) * dk],
                       v_ref[:, hh * dv:(hh + 1) * dv], gate_refs[2 * hh][0], gate_refs[2 * hh + 1][0],
                       *state[3 * hh:3 * hh + 3], sign=1 - 2 * d) for hh in range(hps)]

    @pl.when(d == 0)
    def _():
        rows = pl.ds(pl.multiple_of(c * ln, ln), ln)
        for hh in range(hps):
            hf_scr[rows, hh * dv:(hh + 1) * dv] = hs[hh]

    @pl.when(d == 1)
    def _():
        rows = pl.ds(pl.multiple_of((nc - 1 - c) * ln, ln), ln)
        for hh in range(hps):
            cols = slice(hh * dv, (hh + 1) * dv)
            h = hf_scr[rows, cols] + hs[hh]
            ms = jnp.mean(h * h, axis=-1, keepdims=True)
            y = h * lax.rsqrt(ms + EPS) * g_ref[:, cols]
            out_ref[:, cols] = (y * jax.nn.sigmoid(o_pre_ref[:, cols].astype(F32))).astype(out_ref.dtype)


def _mlstm_scan(qk, p, gates, head_norm, *, batch, heads, v_col, o_col, name="mlstm_scan"):
    t = qk.shape[0]
    s_len = t // batch
    ln = gates.shape[-1]
    nc = s_len // ln
    dk, dv = MLSTM_QK_DIM, MLSTM_V_DIM
    hps = next(n for n in (3, 2, 1) if heads % n == 0)
    groups = heads // hps

    def chunk(d, c):
        return c + d * (nc - 1 - 2 * c)

    def row_block(g, d, c):
        return (g // groups) * nc + chunk(d, c)

    def out_row_block(g, d, c):
        return (g // groups) * nc + nc - 1 - d * c

    def gate_block(kind, hh):
        def index(g, d, c):
            head = (g % groups) * hps + hh
            return (((2 * d + kind) * heads + head) * batch + g // groups) * nc + chunk(d, c), 0, 0
        return index

    wide = hps * dv
    in_specs = [pl.BlockSpec((ln, hps * dk), lambda g, d, c: (row_block(g, d, c), g % groups)),
                pl.BlockSpec((ln, hps * dk), lambda g, d, c: (row_block(g, d, c), groups + g % groups)),
                pl.BlockSpec((ln, wide), lambda g, d, c: (row_block(g, d, c), v_col // wide + g % groups))]
    for hh in range(hps):
        in_specs += [pl.BlockSpec((1, 1, ln), gate_block(0, hh)), pl.BlockSpec((1, 1, ln), gate_block(1, hh))]
    in_specs += [pl.BlockSpec((ln, wide), lambda g, d, c: (out_row_block(g, d, c), o_col // wide + g % groups)),
                 pl.BlockSpec((1, wide), lambda g, d, c: (0, g % groups))]
    state = [pltpu.VMEM((dk, dv), F32), pltpu.VMEM((1, dk), F32), pltpu.VMEM((1, 1), F32)]
    return pl.pallas_call(
        functools.partial(_mlstm_scan_body, hps=hps, nc=nc),
        grid=(batch * groups, 2, nc),
        in_specs=in_specs,
        out_specs=pl.BlockSpec((ln, wide), lambda g, d, c: (out_row_block(g, d, c), g % groups)),
        out_shape=jax.ShapeDtypeStruct((t, heads * dv), BF16),
        scratch_shapes=[pltpu.VMEM((s_len, wide), F32)] + state * hps,
        compiler_params=_params("parallel", "arbitrary", "arbitrary"),
        name=name,
    )(qk, qk, p, *([gates, gates] * hps), p, head_norm.reshape(1, heads * dv).astype(F32))


def _swap_halves(w, axis):
    half = w.shape[axis] // 2
    lo, hi = jnp.split(w, [half], axis=axis)
    return jnp.concatenate([hi, lo], axis=axis)


def _prep_mla_weights(w_in_t, w_uq, q_rank, kv_rank, heads):
    a1 = q_rank + kv_rank
    a2 = a1 + MLA_ROPE_DIM
    w_kr = w_in_t[a1:a2]
    w_in_p = jnp.concatenate([w_in_t[:a1], w_in_t[a2:], w_kr, _swap_halves(w_kr, 0)], axis=0).astype(BF16)
    wq = w_uq.reshape(q_rank, heads, MLA_NOPE_DIM + MLA_ROPE_DIM)
    rope_cols = wq[..., MLA_NOPE_DIM:]
    w_uq_p = jnp.concatenate([wq[..., :MLA_NOPE_DIM], rope_cols, _swap_halves(rope_cols, 2)], axis=-1)
    return w_in_p, w_uq_p.reshape(q_rank, heads * MLA_HEAD_COLS).astype(BF16)


def _rope_tables(positions):
    inv_freq = 1.0 / (ROPE_THETA ** (jnp.arange(0, MLA_ROPE_DIM, 2, dtype=F32) / MLA_ROPE_DIM))
    ang = positions.astype(F32).reshape(-1, 1) * inv_freq
    cos, sin = jnp.cos(ang), jnp.sin(ang)
    k_tab = jnp.concatenate([cos, cos, -sin, sin], axis=1)
    scale = (MLA_NOPE_DIM + MLA_ROPE_DIM) ** -0.5 * math.log2(math.e)
    q_tab = jnp.concatenate([jnp.ones((ang.shape[0], MLA_NOPE_DIM), F32), k_tab], axis=1) * scale
    return q_tab, k_tab


def kernel(x, positions, mem, attn_norm, ffn_norm, mem_norm, final_norm, mla_w_in, mla_q_norm, mla_kv_norm,
           mla_w_uq, mla_w_ukv, mlstm_w_in, mlstm_conv_w, mlstm_gate_b, mlstm_head_norm, w_mem_kv, w_out,
           w_gu, w_down):
    batch, s_len, d_model = x.shape
    depth = attn_norm.shape[0]
    n_mem = mem.shape[1]
    t = batch * s_len
    mem_width = d_model // 4
    main_width = d_model - mem_width
    mla_heads = main_width // MLA_V_DIM
    mlstm_heads = main_width // MLSTM_V_DIM
    qk_width = mlstm_heads * MLSTM_QK_DIM
    v_width = mlstm_heads * MLSTM_V_DIM
    q_rank = mla_q_norm.shape[1]
    kv_rank = mla_kv_norm.shape[1]
    d_ff = w_down.shape[1]

    tm = _pick(t, (1024, 512, 256, 128))
    q_tab, k_tab = _rope_tables(positions)
    mem_n = _rmsnorm(mem.reshape(batch * n_mem, d_model), mem_norm, width=d_model, name="mem_norm")
    mla_w_in_t = jnp.swapaxes(mla_w_in, 1, 2)
    mlstm_w_in_t = jnp.swapaxes(mlstm_w_in, 1, 2)
    h = x.reshape(t, d_model)

    for i in range(depth):
        j = i // N_MIXERS
        hn = _rmsnorm(h, attn_norm[i], width=d_model, name="attn_norm")
        mkv = _matmul(mem_n, w_mem_kv, layer=i, out_dtype=BF16, tm=_pick(batch * n_mem, (512, 256, 128)),
                      tn=_pick(2 * mem_width, (512, 256)), name="mem_kv")
        if i % N_MIXERS == 0:
            w_in_p, w_uq_p = _prep_mla_weights(mla_w_in_t[j], mla_w_uq[j], q_rank, kv_rank, mla_heads)
            n_in = w_in_p.shape[0]
            p = _matmul(hn, w_in_p, out_dtype=F32, tm=tm, tn=_pick(n_in, (896, 768, 640, 512, 384, 256, 128)),
                        name="mla_in_proj", transposed=True, ring=True)
            cq = _rmsnorm(p, mla_q_norm[j], width=q_rank, col_block=0, name="mla_q_norm")
            ckv = _rmsnorm(p, mla_kv_norm[j], width=kv_rank, col_block=q_rank // kv_rank, name="mla_kv_norm")
            head_cols = mla_heads * MLA_HEAD_COLS
            tm_small_k = _pick(t, (2048, 1024, 512, 256, 128))
            q = _matmul_mul(cq, w_uq_p, q_tab, out_dtype=BF16, tm=tm_small_k,
                            tn=_pick(head_cols, (1024, 768, 512, 256)), name="mla_q_proj")
            kv = _matmul(ckv, mla_w_ukv, layer=j, out_dtype=BF16, tm=tm_small_k,
                         tn=_pick(head_cols, (1536, 1024, 768, 512, 256)), name="mla_kv_proj")
            q_mem_src, q_mem_col = p, q_rank + kv_rank
            kr = _krope(p, k_tab, col_block=(q_mem_col + mem_width) // (2 * MLA_ROPE_DIM))
            main = _flash_attention(q, kv, kr, batch=batch, heads=mla_heads)
        else:
            g_col = 2 * qk_width + 2 * v_width
            w_gates_t = mlstm_w_in_t[j, g_col:g_col + 4 * mlstm_heads]
            w_q_mem_t = mlstm_w_in_t[j, g_col + 4 * mlstm_heads:]
            p = _matmul(hn, mlstm_w_in_t, layer=j, n=g_col, out_dtype=BF16, tm=tm, tn=_pick(g_col, (512, 256, 128)),
                        name="mlstm_in_proj", transposed=True, ring=True)
            q_mem_src = _matmul(hn, w_q_mem_t, out_dtype=BF16, tm=tm, tn=_pick(mem_width, (512, 256, 128)),
                                name="mlstm_qmem_proj", transposed=True)
            q_mem_col = 0
            gates = _mlstm_gates(w_gates_t, hn, mlstm_gate_b[j], heads=mlstm_heads)
            gates = gates.reshape(-1, 1, min(MLSTM_CHUNK, s_len))
            qk = _mlstm_conv(p, mlstm_conv_w[j], batch=batch, width=2 * qk_width, q_width=qk_width)
            main = _mlstm_scan(qk, p, gates, mlstm_head_norm[j], batch=batch, heads=mlstm_heads,
                               v_col=2 * qk_width, o_col=2 * qk_width + v_width)
        mem_out = _memory_attention(q_mem_src, mkv, batch=batch, q_col=q_mem_col, width=mem_width)
        h = _matmul_res2(main, mem_out, w_out, h, layer=i, tm=tm, tn=_pick(d_model, (512, 256)), name="out_proj",
                         ring=True)
        hn = _rmsnorm(h, ffn_norm[i], width=d_model, name="ffn_norm")
        act, w_down_bf16 = _matmul_swiglu(hn, w_gu, layer=i, tm=tm, tn=_pick(d_ff, (256, 128)), name="ffn_up",
                                          ring=True, side=(w_down, i))
        h = _matmul_res(act, w_down_bf16[None], h, layer=0, tm=_pick(t, (512, 256, 128)),
                        tn=_pick(d_model, (512, 256)), name="ffn_down")
    out = _rmsnorm(h, final_norm, width=d_model, out_dtype=x.dtype, name="final_norm")
    return out.reshape(batch, s_len, d_model)
```

```python
import functools
import math

import jax
import jax.numpy as jnp
from jax import lax
from jax.experimental import pallas as pl
from jax.experimental.pallas import tpu as pltpu

F32 = jnp.float32
BF16 = jnp.bfloat16

EPS = 1e-6
ROPE_THETA = 10000.0
MLA_NOPE_DIM = 128
MLA_ROPE_DIM = 64
MLA_V_DIM = 128
MLA_HEAD_COLS = 2 * MLA_NOPE_DIM
MLSTM_V_DIM = 512
MLSTM_QK_DIM = MLSTM_V_DIM // 2
MLSTM_CHUNK = 512
MEM_HEADS = 4
IGATE_CAP = 15.0
STAB_INIT = -1e30
N_MIXERS = 2

V7X_SUBLANES = 8
V7X_VMEM_BYTES = 64 * 1024 * 1024
VMEM_LIMIT_BYTES = V7X_VMEM_BYTES - 8 * 1024 * 1024


def _params(*semantics):
    return pltpu.CompilerParams(dimension_semantics=semantics, vmem_limit_bytes=VMEM_LIMIT_BYTES)


def _tile_rows(dtype):
    return V7X_SUBLANES * (4 // jnp.dtype(dtype).itemsize)


def _pick(n, candidates):
    for c in candidates:
        if n % c == 0:
            return c
    return n


def _dot(a, b):
    return jnp.dot(a, b, preferred_element_type=F32)


def _dot_nt(a, b):
    return lax.dot_general(a, b, (((1,), (1,)), ((), ())), preferred_element_type=F32)


def _dot_tn(a, b):
    return lax.dot_general(a, b, (((0,), (0,)), ((), ())), preferred_element_type=F32)


def _rmsnorm_body(x_ref, g_ref, o_ref):
    x = x_ref[...].astype(F32)
    ms = jnp.mean(x * x, axis=-1, keepdims=True)
    o_ref[...] = (x * lax.rsqrt(ms + EPS) * g_ref[...]).astype(o_ref.dtype)


def _rmsnorm(x, g, *, width, col_block=0, out_dtype=BF16, name="rmsnorm"):
    t = x.shape[0]
    tm = _pick(t, (512, 256, 128, 64, 8))
    return pl.pallas_call(
        _rmsnorm_body,
        grid=(t // tm,),
        in_specs=[pl.BlockSpec((tm, width), lambda i: (i, col_block)),
                  pl.BlockSpec((1, width), lambda i: (0, 0))],
        out_specs=pl.BlockSpec((tm, width), lambda i: (i, 0)),
        out_shape=jax.ShapeDtypeStruct((t, width), out_dtype),
        compiler_params=_params("parallel"),
        name=name,
    )(x, g.reshape(1, width).astype(F32))


def _ep_plain(mm, a, w, extra, o_ref):
    o_ref[...] = mm(a[0], w[0][...]).astype(o_ref.dtype)


def _ep_mul(mm, a, w, extra, o_ref):
    table = extra[0][...]
    reps = o_ref.shape[1] // table.shape[1]
    if reps > 1:
        table = jnp.concatenate([table] * reps, axis=1)
    o_ref[...] = (mm(a[0], w[0][...]) * table).astype(o_ref.dtype)


def _ep_residual(mm, a, w, extra, o_ref):
    acc = mm(a[0], w[0][...])
    for a_val, w_ref in zip(a[1:], w[1:]):
        acc = acc + mm(a_val, w_ref[...])
    o_ref[...] = extra[0][...] + acc


def _ep_swiglu(mm, a, w, extra, o_ref):
    g = mm(a[0], w[0][...])
    u = mm(a[0], w[1][...])
    o_ref[...] = (g * jax.nn.sigmoid(g) * u).astype(o_ref.dtype)


RING_SLOTS = 3


def _ring_fetch(a_hbm, bufs, sems, *, tm, n_m, total):
    step = pl.program_id(0) * n_m + pl.program_id(1)

    def copy(op, st):
        row = (st % n_m) * tm
        if not isinstance(row, int):
            row = pl.multiple_of(row, tm)
        slot = st % RING_SLOTS
        return pltpu.make_async_copy(a_hbm[op].at[pl.ds(row, tm), :], bufs[op].at[slot], sems.at[op, slot])

    @pl.when(step == 0)
    def _():
        for op in range(len(a_hbm)):
            for st in range(min(RING_SLOTS - 1, total)):
                copy(op, st).start()

    @pl.when(step + (RING_SLOTS - 1) < total)
    def _():
        for op in range(len(a_hbm)):
            copy(op, step + (RING_SLOTS - 1)).start()

    for op in range(len(a_hbm)):
        copy(op, step).wait()
    slot = step % RING_SLOTS
    return [buf[slot] for buf in bufs]


def _mm_body(*refs, n_a, n_w, n_extra, epilogue, cast, transposed, ring, side):
    a = refs[:n_a]
    w = refs[n_a:n_a + n_w]
    extra = refs[n_a + n_w:n_a + n_w + n_extra]
    n_in = n_a + n_w + n_extra
    o_ref = refs[n_in]
    scratch = refs[n_in + 1:]
    if side:
        refs[n_in + 2][...] = refs[n_in][...].astype(BF16)
        o_ref = refs[n_in + 1]
        scratch = refs[n_in + 3:]
    if cast:
        w_scratch, scratch = scratch[:n_w], scratch[n_w:]

        @pl.when(pl.program_id(1) == 0)
        def _():
            for w_ref, s_ref in zip(w, w_scratch):
                s_ref[...] = w_ref[...].astype(BF16)

        w = w_scratch
    if ring is None:
        a_vals = [a_ref[...] for a_ref in a]
    else:
        a_vals = _ring_fetch(a, scratch[:n_a], scratch[n_a], **ring)
    epilogue(_dot_nt if transposed else _dot, a_vals, w, extra, o_ref)


def _mm(name, epilogue, a_ops, w_ops, extra_ops, *, m, n, tm, tn, out_dtype, transposed=False, ring=False,
        side=None):
    e_specs = [pl.BlockSpec(bs, im) for _, bs, im in extra_ops]
    n_m = m // tm
    out_specs = pl.BlockSpec((tm, tn), lambda j, i: (i, j))
    out_shape = jax.ShapeDtypeStruct((m, n), out_dtype)
    side_ops = []
    if side is not None:
        side_arr, side_layer = side
        _, side_rows, side_cols = side_arr.shape
        slab = side_rows // ((n // tn) * n_m)
        assert slab * (n // tn) * n_m == side_rows and slab % _tile_rows(BF16) == 0
        e_specs.append(pl.BlockSpec((None, slab, side_cols), lambda j, i: (side_layer, j * n_m + i, 0)))
        out_specs = [out_specs, pl.BlockSpec((slab, side_cols), lambda j, i: (j * n_m + i, 0))]
        out_shape = [out_shape, jax.ShapeDtypeStruct((side_rows, side_cols), BF16)]
        side_ops = [side_arr]
    ring_args = None
    ring_scratch = []
    if ring:
        a_specs = [pl.BlockSpec(memory_space=pl.ANY) for _ in a_ops]
        ring_args = dict(tm=tm, n_m=n_m, total=(n // tn) * n_m)
        ring_scratch = [pltpu.VMEM((RING_SLOTS,) + tuple(bs), arr.dtype) for arr, bs, _ in a_ops]
        ring_scratch.append(pltpu.SemaphoreType.DMA((len(a_ops), RING_SLOTS)))
    else:
        a_specs = [pl.BlockSpec(bs, im) for _, bs, im in a_ops]

    def w_spec(layer, rows, rb, co):
        if transposed:
            return pl.BlockSpec((None, tn, rows), lambda j, i: (layer, j + co, rb))
        return pl.BlockSpec((None, rows, tn), lambda j, i: (layer, rb, j + co))

    w_specs = [w_spec(layer, rows, rb, co) for _, layer, rows, rb, co in w_ops]
    cast = w_ops[0][0].dtype != BF16
    scratch = []
    if cast:
        scratch = [pltpu.VMEM((tn, rows) if transposed else (rows, tn), BF16) for _, _, rows, _, _ in w_ops]
    return pl.pallas_call(
        functools.partial(_mm_body, n_a=len(a_ops), n_w=len(w_ops), n_extra=len(extra_ops),
                          epilogue=epilogue, cast=cast, transposed=transposed, ring=ring_args,
                          side=side is not None),
        grid=(n // tn, n_m),
        in_specs=a_specs + w_specs + e_specs,
        out_specs=out_specs,
        out_shape=out_shape,
        scratch_shapes=scratch + ring_scratch,
        compiler_params=_params("arbitrary", "arbitrary"),
        name=name,
    )(*[op[0] for op in a_ops], *[op[0] for op in w_ops], *[op[0] for op in extra_ops], *side_ops)


def _rows(tm, k, col_block=0):
    return (tm, k), lambda j, i: (i, col_block)


def _as3d(w):
    return w if w.ndim == 3 else w[None]


def _matmul(a, w, *, layer=0, n=None, out_dtype, tm, tn, name, transposed=False, ring=False):
    w = _as3d(w)
    m, k = a.shape
    if n is None:
        n = w.shape[1] if transposed else w.shape[2]
    return _mm(name, _ep_plain, [(a, *_rows(tm, k))], [(w, layer, k, 0, 0)], [],
               m=m, n=n, tm=tm, tn=tn, out_dtype=out_dtype, transposed=transposed, ring=ring)


def _matmul_mul(a, w, table, *, out_dtype, tm, tn, name):
    w = _as3d(w)
    m, k = a.shape
    c = table.shape[1]
    return _mm(name, _ep_mul, [(a, *_rows(tm, k))], [(w, 0, k, 0, 0)], [(table, (tm, c), lambda j, i: (i, 0))],
               m=m, n=w.shape[2], tm=tm, tn=tn, out_dtype=out_dtype)


def _matmul_res(a, w, res, *, layer, tm, tn, name):
    m, k = a.shape
    return _mm(name, _ep_residual, [(a, *_rows(tm, k))], [(w, layer, k, 0, 0)],
               [(res, (tm, tn), lambda j, i: (i, j))], m=m, n=w.shape[2], tm=tm, tn=tn, out_dtype=F32)


def _matmul_res2(a1, a2, w, res, *, layer, tm, tn, name, ring=False):
    m, k1 = a1.shape
    k2 = a2.shape[1]
    assert k1 % k2 == 0
    return _mm(name, _ep_residual, [(a1, *_rows(tm, k1)), (a2, *_rows(tm, k2))],
               [(w, layer, k1, 0, 0), (w, layer, k2, k1 // k2, 0)],
               [(res, (tm, tn), lambda j, i: (i, j))], m=m, n=w.shape[2], tm=tm, tn=tn, out_dtype=F32, ring=ring)


def _matmul_swiglu(a, w_gu, *, layer, tm, tn, name, ring=False, side=None):
    m, k = a.shape
    f = w_gu.shape[2] // 2
    return _mm(name, _ep_swiglu, [(a, *_rows(tm, k))], [(w_gu, layer, k, 0, 0), (w_gu, layer, k, 0, f // tn)], [],
               m=m, n=f, tm=tm, tn=tn, out_dtype=BF16, ring=ring, side=side)


def _krope_body(x_ref, t_ref, o_ref):
    y = x_ref[...] * t_ref[...]
    o_ref[...] = (y + pltpu.roll(y, MLA_ROPE_DIM, 1)).astype(o_ref.dtype)


def _krope(p, table, *, col_block, name="mla_krope"):
    t = p.shape[0]
    tm = _pick(t, (1024, 512, 256, 128, 8))
    w = 2 * MLA_ROPE_DIM
    return pl.pallas_call(
        _krope_body,
        grid=(t // tm,),
        in_specs=[pl.BlockSpec((tm, w), lambda i: (i, col_block)),
                  pl.BlockSpec((tm, w), lambda i: (i, 0))],
        out_specs=pl.BlockSpec((tm, w), lambda i: (i, 0)),
        out_shape=jax.ShapeDtypeStruct((t, w), BF16),
        compiler_params=_params("parallel"),
        name=name,
    )(p, table)


def _flash_body(q_ref, kn_ref, v_ref, kr_ref, o_ref, k_scr, *, tk):
    @pl.when(pl.program_id(2) == 0)
    def _():
        k_scr[:, :MLA_NOPE_DIM] = kn_ref[...]
        k_scr[:, MLA_NOPE_DIM:] = kr_ref[...]

    q = q_ref[...]
    tq = q.shape[0]
    s_len = k_scr.shape[0]
    m = jnp.full((tq, 1), -jnp.inf, F32)
    l = jnp.zeros((tq, 1), F32)
    acc = jnp.zeros((tq, MLA_V_DIM), F32)
    for c in range(s_len // tk):
        s = _dot_nt(q, k_scr[c * tk:(c + 1) * tk, :])
        m_new = jnp.maximum(m, jnp.max(s, axis=-1, keepdims=True))
        alpha = jnp.exp2(m - m_new)
        p = jnp.exp2(s - m_new)
        l = alpha * l + jnp.sum(p, axis=-1, keepdims=True)
        acc = alpha * acc + _dot(p.astype(BF16), v_ref[c * tk:(c + 1) * tk, :])
        m = m_new
    o_ref[...] = (acc / l).astype(o_ref.dtype)


def _flash_attention(q, kv, kr, *, batch, heads, name="mla_flash"):
    t = q.shape[0]
    s_len = t // batch
    tq = _pick(s_len, (2048, 1024, 512, 256, 128))
    tk = _pick(s_len, (1024, 512, 256, 128))
    nq = s_len // tq
    return pl.pallas_call(
        functools.partial(_flash_body, tk=tk),
        grid=(batch, heads, nq),
        in_specs=[pl.BlockSpec((tq, MLA_HEAD_COLS), lambda b, h, i: (b * nq + i, h)),
                  pl.BlockSpec((s_len, MLA_NOPE_DIM), lambda b, h, i: (b, 2 * h)),
                  pl.BlockSpec((s_len, MLA_V_DIM), lambda b, h, i: (b, 2 * h + 1)),
                  pl.BlockSpec((s_len, 2 * MLA_ROPE_DIM), lambda b, h, i: (b, 0))],
        out_specs=pl.BlockSpec((tq, MLA_V_DIM), lambda b, h, i: (b * nq + i, h)),
        out_shape=jax.ShapeDtypeStruct((t, heads * MLA_V_DIM), BF16),
        scratch_shapes=[pltpu.VMEM((s_len, MLA_HEAD_COLS), BF16)],
        compiler_params=_params("parallel", "parallel", "arbitrary"),
        name=name,
    )(q, kv, kv, kr)


def _memattn_body(*refs, scale):
    q_refs = refs[:MEM_HEADS]
    mk_ref, mv_ref, o_ref = refs[MEM_HEADS:]
    dh = q_refs[0].shape[1]
    for h in range(MEM_HEADS):
        q = (q_refs[h][...].astype(F32) * scale).astype(BF16)
        s = _dot_nt(q, mk_ref[:, h * dh:(h + 1) * dh])
        p = jnp.exp(s - jnp.max(s, axis=-1, keepdims=True))
        l = jnp.sum(p, axis=-1, keepdims=True)
        o = _dot((p / l).astype(BF16), mv_ref[:, h * dh:(h + 1) * dh])
        o_ref[:, h * dh:(h + 1) * dh] = o.astype(o_ref.dtype)


def _memory_attention(p, mkv, *, batch, q_col, width, name="mem_attn"):
    t = p.shape[0]
    s_len = t // batch
    n_mem = mkv.shape[0] // batch
    dh = width // MEM_HEADS
    tq = _pick(s_len, (512, 256, 128))
    nq = s_len // tq
    q_specs = [pl.BlockSpec((tq, dh), functools.partial(lambda b, i, h: (b * nq + i, q_col // dh + h), h=h))
               for h in range(MEM_HEADS)]
    return pl.pallas_call(
        functools.partial(_memattn_body, scale=dh ** -0.5),
        grid=(batch, nq),
        in_specs=q_specs + [pl.BlockSpec((n_mem, width), lambda b, i: (b, 0)),
                            pl.BlockSpec((n_mem, width), lambda b, i: (b, 1))],
        out_specs=pl.BlockSpec((tq, width), lambda b, i: (b * nq + i, 0)),
        out_shape=jax.ShapeDtypeStruct((t, width), BF16),
        compiler_params=_params("parallel", "parallel"),
        name=name,
    )(*([p] * MEM_HEADS), mkv, mkv)


def _gates_body(w_ref, x_ref, b_ref, f_ref, o_ref):
    g = _dot_nt(w_ref[...].astype(BF16), x_ref[...]) + b_ref[...]
    igate = IGATE_CAP * jnp.tanh(g / IGATE_CAP)
    log_fgate = jnp.minimum(g, 0.0) - jnp.log1p(jnp.exp(-jnp.abs(g)))
    o_ref[...] = jnp.where(f_ref[...] > 0.5, log_fgate, igate) * math.log2(math.e)


def _mlstm_gates(w_t, hn, bias, *, heads, name="mlstm_gates"):
    rows, k = w_t.shape
    t = hn.shape[0]
    tn = _pick(t, (1024, 512, 256, 128))
    is_f = (jnp.arange(rows) // heads) % 2 == 1
    return pl.pallas_call(
        _gates_body,
        grid=(t // tn,),
        in_specs=[pl.BlockSpec((rows, k), lambda i: (0, 0)),
                  pl.BlockSpec((tn, k), lambda i: (i, 0)),
                  pl.BlockSpec((rows, 1), lambda i: (0, 0)),
                  pl.BlockSpec((rows, 1), lambda i: (0, 0))],
        out_specs=pl.BlockSpec((rows, tn), lambda i: (0, i)),
        out_shape=jax.ShapeDtypeStruct((rows, t), F32),
        compiler_params=_params("parallel"),
        name=name,
    )(w_t, hn, bias.reshape(rows, 1).astype(F32), is_f.astype(F32).reshape(rows, 1))


def _conv_body(x_ref, prev_ref, next_ref, w_ref, o_ref, *, blocks_per_seq, q_blocks, q_scale):
    i = pl.program_id(0)
    j = pl.program_id(1)
    x = x_ref[...].astype(F32)
    ts = x.shape[0]
    pos = i % blocks_per_seq
    halo = prev_ref.shape[0]
    prev_row = jnp.where(pos == 0, 0.0, prev_ref[...].astype(F32)[halo - 1:halo, :])
    next_row = jnp.where(pos == blocks_per_seq - 1, 0.0, next_ref[...].astype(F32)[0:1, :])
    row = lax.broadcasted_iota(jnp.int32, x.shape, 0)
    x_prev = jnp.where(row == 0, prev_row, pltpu.roll(x, 1, 0))
    x_next = jnp.where(row == ts - 1, next_row, pltpu.roll(x, ts - 1, 0))
    y = x_prev * w_ref[0:1, :] + x * w_ref[1:2, :] + x_next * w_ref[2:3, :]
    y = y * jax.nn.sigmoid(y)
    scale = jnp.where(j < q_blocks, q_scale, 1.0)
    o_ref[...] = (y * scale).astype(o_ref.dtype)


def _mlstm_conv(p, conv_w, *, batch, width, q_width, name="mlstm_conv"):
    t = p.shape[0]
    s_len = t // batch
    ts = _pick(s_len, (512, 256, 128))
    tc = _pick(q_width, (1536, 1024, 512, 256, 128))
    halo = _tile_rows(p.dtype)
    hb = ts // halo
    return pl.pallas_call(
        functools.partial(_conv_body, blocks_per_seq=s_len // ts, q_blocks=q_width // tc,
                          q_scale=MLSTM_QK_DIM ** -0.5),
        grid=(t // ts, width // tc),
        in_specs=[pl.BlockSpec((ts, tc), lambda i, j: (i, j)),
                  pl.BlockSpec((halo, tc), lambda i, j: (jnp.maximum(i * hb - 1, 0), j)),
                  pl.BlockSpec((halo, tc), lambda i, j: (jnp.minimum((i + 1) * hb, t // halo - 1), j)),
                  pl.BlockSpec((3, tc), lambda i, j: (0, j))],
        out_specs=pl.BlockSpec((ts, tc), lambda i, j: (i, j)),
        out_shape=jax.ShapeDtypeStruct((t, width), BF16),
        compiler_params=_params("parallel", "parallel"),
        name=name,
    )(p, p, p, conv_w.astype(F32))


def _mlstm_chunk(q, k, v, ig, lf, c_scr, n_scr, m_scr, *, sign):
    ln = q.shape[0]
    t_idx = lax.broadcasted_iota(jnp.int32, (ln, ln), 0)
    j_idx = lax.broadcasted_iota(jnp.int32, (ln, ln), 1)
    mask = (j_idx - t_idx) * sign <= 0
    eye = j_idx == t_idx
    lf_b = jnp.broadcast_to(lf, (ln, ln))
    ig_b = jnp.broadcast_to(ig, (ln, ln))
    cs_col = jnp.sum(jnp.where(mask, lf_b, 0.0), axis=1, keepdims=True)
    cs_row = jnp.sum(jnp.where(eye, cs_col, 0.0), axis=0, keepdims=True)
    ig_col = jnp.sum(jnp.where(eye, ig_b, 0.0), axis=1, keepdims=True)
    m_prev = m_scr[...]

    dmat = jnp.where(mask, cs_col - cs_row + ig, -jnp.inf)
    inter = cs_col + m_prev
    m_row = jnp.maximum(jnp.max(dmat, axis=1, keepdims=True), inter)
    s = _dot_nt(q, k) * jnp.exp2(dmat - m_row)
    w_inter = jnp.exp2(inter - m_row)
    num = _dot(s.astype(BF16), v) + w_inter * _dot(q, c_scr[...].astype(BF16))
    qn = jnp.sum(q.astype(F32) * n_scr[...], axis=1, keepdims=True)
    den = jnp.sum(s, axis=1, keepdims=True) + w_inter * qn
    h = num / jnp.maximum(jnp.abs(den), jnp.exp2(-m_row))

    b_end = jnp.sum(lf, axis=1, keepdims=True)
    w_tok_row = b_end - cs_row + ig
    w_tok_col = b_end - cs_col + ig_col
    m_new = jnp.maximum(b_end + m_prev, jnp.max(w_tok_row, axis=1, keepdims=True))
    decay = jnp.exp2(b_end + m_prev - m_new)
    wk = k.astype(F32) * jnp.exp2(w_tok_col - m_new)
    c_scr[...] = decay * c_scr[...] + _dot_tn(wk.astype(BF16), v)
    n_scr[...] = decay * n_scr[...] + jnp.sum(wk, axis=0, keepdims=True)
    m_scr[...] = m_new
    return h


def _mlstm_scan_body(*refs, hps, nc):
    q_ref, k_ref, v_ref = refs[:3]
    gate_refs = refs[3:3 + 2 * hps]
    o_pre_ref, g_ref, out_ref = refs[3 + 2 * hps:6 + 2 * hps]
    hf_scr = refs[6 + 2 * hps]
    state = refs[7 + 2 * hps:]
    dk, dv = MLSTM_QK_DIM, MLSTM_V_DIM
    ln = q_ref.shape[0]
    d = pl.program_id(1)
    c = pl.program_id(2)

    @pl.when(c == 0)
    def _():
        for idx in range(0, len(state), 3):
            c_scr, n_scr, m_scr = state[idx:idx + 3]
            c_scr[...] = jnp.zeros_like(c_scr)
            n_scr[...] = jnp.zeros_like(n_scr)
            m_scr[...] = jnp.full_like(m_scr, STAB_INIT)

    hs = [_mlstm_chunk(q_ref[:, hh * dk:(hh + 1) * dk], k_ref[:, hh * dk:(hh + 1) * dk],
                       v_ref[:, hh * dv:(hh + 1) * dv], gate_refs[2 * hh][0], gate_refs[2 * hh + 1][0],
                       *state[3 * hh:3 * hh + 3], sign=1 - 2 * d) for hh in range(hps)]

    @pl.when(d == 0)
    def _():
        rows = pl.ds(pl.multiple_of(c * ln, ln), ln)
        for hh in range(hps):
            hf_scr[rows, hh * dv:(hh + 1) * dv] = hs[hh]

    @pl.when(d == 1)
    def _():
        rows = pl.ds(pl.multiple_of((nc - 1 - c) * ln, ln), ln)
        for hh in range(hps):
            cols = slice(hh * dv, (hh + 1) * dv)
            h = hf_scr[rows, cols] + hs[hh]
            ms = jnp.mean(h * h, axis=-1, keepdims=True)
            y = h * lax.rsqrt(ms + EPS) * g_ref[:, cols]
            out_ref[:, cols] = (y * jax.nn.sigmoid(o_pre_ref[:, cols].astype(F32))).astype(out_ref.dtype)


def _mlstm_scan(qk, p, gates, head_norm, *, batch, heads, v_col, o_col, name="mlstm_scan"):
    t = qk.shape[0]
    s_len = t // batch
    ln = gates.shape[-1]
    nc = s_len // ln
    dk, dv = MLSTM_QK_DIM, MLSTM_V_DIM
    hps = next(n for n in (3, 2, 1) if heads % n == 0)
    groups = heads // hps

    def chunk(d, c):
        return c + d * (nc - 1 - 2 * c)

    def row_block(g, d, c):
        return (g // groups) * nc + chunk(d, c)

    def out_row_block(g, d, c):
        return (g // groups) * nc + nc - 1 - d * c

    def gate_block(kind, hh):
        def index(g, d, c):
            head = (g % groups) * hps + hh
            return (((2 * d + kind) * heads + head) * batch + g // groups) * nc + chunk(d, c), 0, 0
        return index

    wide = hps * dv
    in_specs = [pl.BlockSpec((ln, hps * dk), lambda g, d, c: (row_block(g, d, c), g % groups)),
                pl.BlockSpec((ln, hps * dk), lambda g, d, c: (row_block(g, d, c), groups + g % groups)),
                pl.BlockSpec((ln, wide), lambda g, d, c: (row_block(g, d, c), v_col // wide + g % groups))]
    for hh in range(hps):
        in_specs += [pl.BlockSpec((1, 1, ln), gate_block(0, hh)), pl.BlockSpec((1, 1, ln), gate_block(1, hh))]
    in_specs += [pl.BlockSpec((ln, wide), lambda g, d, c: (out_row_block(g, d, c), o_col // wide + g % groups)),
                 pl.BlockSpec((1, wide), lambda g, d, c: (0, g % groups))]
    state = [pltpu.VMEM((dk, dv), F32), pltpu.VMEM((1, dk), F32), pltpu.VMEM((1, 1), F32)]
    return pl.pallas_call(
        functools.partial(_mlstm_scan_body, hps=hps, nc=nc),
        grid=(batch * groups, 2, nc),
        in_specs=in_specs,
        out_specs=pl.BlockSpec((ln, wide), lambda g, d, c: (out_row_block(g, d, c), g % groups)),
        out_shape=jax.ShapeDtypeStruct((t, heads * dv), BF16),
        scratch_shapes=[pltpu.VMEM((s_len, wide), F32)] + state * hps,
        compiler_params=_params("parallel", "arbitrary", "arbitrary"),
        name=name,
    )(qk, qk, p, *([gates, gates] * hps), p, head_norm.reshape(1, heads * dv).astype(F32))


def _swap_halves(w, axis):
    half = w.shape[axis] // 2
    lo, hi = jnp.split(w, [half], axis=axis)
    return jnp.concatenate([hi, lo], axis=axis)


def _prep_mla_weights(w_in_t, w_uq, q_rank, kv_rank, heads):
    a1 = q_rank + kv_rank
    a2 = a1 + MLA_ROPE_DIM
    w_kr = w_in_t[a1:a2]
    w_mem_rope = jnp.concatenate([w_in_t[a2:], w_kr, _swap_halves(w_kr, 0)], axis=0).astype(BF16)
    wq = w_uq.reshape(q_rank, heads, MLA_NOPE_DIM + MLA_ROPE_DIM)
    rope_cols = wq[..., MLA_NOPE_DIM:]
    w_uq_p = jnp.concatenate([wq[..., :MLA_NOPE_DIM], rope_cols, _swap_halves(rope_cols, 2)], axis=-1)
    return w_mem_rope, w_uq_p.reshape(q_rank, heads * MLA_HEAD_COLS).astype(BF16)


def _rope_tables(positions):
    inv_freq = 1.0 / (ROPE_THETA ** (jnp.arange(0, MLA_ROPE_DIM, 2, dtype=F32) / MLA_ROPE_DIM))
    ang = positions.astype(F32).reshape(-1, 1) * inv_freq
    cos, sin = jnp.cos(ang), jnp.sin(ang)
    k_tab = jnp.concatenate([cos, cos, -sin, sin], axis=1)
    scale = (MLA_NOPE_DIM + MLA_ROPE_DIM) ** -0.5 * math.log2(math.e)
    q_tab = jnp.concatenate([jnp.ones((ang.shape[0], MLA_NOPE_DIM), F32), k_tab], axis=1) * scale
    return q_tab, k_tab


def kernel(x, positions, mem, attn_norm, ffn_norm, mem_norm, final_norm, mla_w_in, mla_q_norm, mla_kv_norm,
           mla_w_uq, mla_w_ukv, mlstm_w_in, mlstm_conv_w, mlstm_gate_b, mlstm_head_norm, w_mem_kv, w_out,
           w_gu, w_down):
    batch, s_len, d_model = x.shape
    depth = attn_norm.shape[0]
    n_mem = mem.shape[1]
    t = batch * s_len
    mem_width = d_model // 4
    main_width = d_model - mem_width
    mla_heads = main_width // MLA_V_DIM
    mlstm_heads = main_width // MLSTM_V_DIM
    qk_width = mlstm_heads * MLSTM_QK_DIM
    v_width = mlstm_heads * MLSTM_V_DIM
    q_rank = mla_q_norm.shape[1]
    kv_rank = mla_kv_norm.shape[1]
    d_ff = w_down.shape[1]

    tm = _pick(t, (1024, 512, 256, 128))
    q_tab, k_tab = _rope_tables(positions)
    mem_n = _rmsnorm(mem.reshape(batch * n_mem, d_model), mem_norm, width=d_model, name="mem_norm")
    mla_w_in_t = jnp.swapaxes(mla_w_in, 1, 2)
    mlstm_w_in_t = jnp.swapaxes(mlstm_w_in, 1, 2)
    h = x.reshape(t, d_model)

    for i in range(depth):
        j = i // N_MIXERS
        hn = _rmsnorm(h, attn_norm[i], width=d_model, name="attn_norm")
        mkv = _matmul(mem_n, w_mem_kv, layer=i, out_dtype=BF16, tm=_pick(batch * n_mem, (512, 256, 128)),
                      tn=_pick(2 * mem_width, (512, 256)), name="mem_kv")
        if i % N_MIXERS == 0:
            w_mem_rope, w_uq_p = _prep_mla_weights(mla_w_in_t[j], mla_w_uq[j], q_rank, kv_rank, mla_heads)
            latent = q_rank + kv_rank
            p_lat = _matmul(hn, mla_w_in_t, layer=j, n=latent, out_dtype=F32, tm=tm,
                            tn=_pick(latent, (512, 256, 128)), name="mla_latent_proj", transposed=True, ring=True)
            p = _matmul(hn, w_mem_rope, out_dtype=F32, tm=tm, tn=w_mem_rope.shape[0], name="mla_mem_rope_proj",
                        transposed=True, ring=True)
            cq = _rmsnorm(p_lat, mla_q_norm[j], width=q_rank, col_block=0, name="mla_q_norm")
            ckv = _rmsnorm(p_lat, mla_kv_norm[j], width=kv_rank, col_block=q_rank // kv_rank, name="mla_kv_norm")
            head_cols = mla_heads * MLA_HEAD_COLS
            tm_small_k = _pick(t, (2048, 1024, 512, 256, 128))
            q = _matmul_mul(cq, w_uq_p, q_tab, out_dtype=BF16, tm=tm_small_k,
                            tn=_pick(head_cols, (1024, 768, 512, 256)), name="mla_q_proj")
            kv = _matmul(ckv, mla_w_ukv, layer=j, out_dtype=BF16, tm=tm_small_k,
                         tn=_pick(head_cols, (1536, 1024, 768, 512, 256)), name="mla_kv_proj")
            q_mem_src, q_mem_col = p, 0
            kr = _krope(p, k_tab, col_block=mem_width // (2 * MLA_ROPE_DIM))
            main = _flash_attention(q, kv, kr, batch=batch, heads=mla_heads)
        else:
            g_col = 2 * qk_width + 2 * v_width
            w_gates_t = mlstm_w_in_t[j, g_col:g_col + 4 * mlstm_heads]
            w_q_mem_t = mlstm_w_in_t[j, g_col + 4 * mlstm_heads:]
            p = _matmul(hn, mlstm_w_in_t, layer=j, n=g_col, out_dtype=BF16, tm=tm, tn=_pick(g_col, (512, 256, 128)),
                        name="mlstm_in_proj", transposed=True, ring=True)
            q_mem_src = _matmul(hn, w_q_mem_t, out_dtype=BF16, tm=tm, tn=_pick(mem_width, (512, 256, 128)),
                                name="mlstm_qmem_proj", transposed=True)
            q_mem_col = 0
            gates = _mlstm_gates(w_gates_t, hn, mlstm_gate_b[j], heads=mlstm_heads)
            gates = gates.reshape(-1, 1, min(MLSTM_CHUNK, s_len))
            qk = _mlstm_conv(p, mlstm_conv_w[j], batch=batch, width=2 * qk_width, q_width=qk_width)
            main = _mlstm_scan(qk, p, gates, mlstm_head_norm[j], batch=batch, heads=mlstm_heads,
                               v_col=2 * qk_width, o_col=2 * qk_width + v_width)
        mem_out = _memory_attention(q_mem_src, mkv, batch=batch, q_col=q_mem_col, width=mem_width)
        h = _matmul_res2(main, mem_out, w_out, h, layer=i, tm=tm, tn=_pick(d_model, (512, 256)), name="out_proj",
                         ring=True)
        hn = _rmsnorm(h, ffn_norm[i], width=d_model, name="ffn_norm")
        act, w_down_bf16 = _matmul_swiglu(hn, w_gu, layer=i, tm=tm, tn=_pick(d_ff, (256, 128)), name="ffn_up",
                                          ring=True, side=(w_down, i))
        h = _matmul_res(act, w_down_bf16[None], h, layer=0, tm=_pick(t, (512, 256, 128)),
                        tn=_pick(d_model, (512, 256)), name="ffn_down")
    out = _rmsnorm(h, final_norm, width=d_model, out_dtype=x.dtype, name="final_norm")
    return out.reshape(batch, s_len, d_model)
```
